```python
import jax
import jax.numpy as jnp
from jax import lax
import numpy as np


D_MODEL = 2048
BATCH = 4
SEQ = 4096
DEPTH = 2

N_MIXERS = 2
EPS = 1e-6
HG_DK = 128
HG_HEADS = D_MODEL // HG_DK
HG_DV = D_MODEL // HG_HEADS
HG_CHUNK = 64
ATT_HEAD_DIM = 128
ATT_HEADS = D_MODEL // ATT_HEAD_DIM
DIL_BRANCHES = ((128, 1), (512, 4), (2048, 16))
ROPE_THETA = 10000.0
NEG_INF = -1e30
FFN_DENSE = 5504
N_EXPERTS = 8
TOP_K = 2
FFN_EXPERT = 7168
MOE_BLOCK = 512
N_HGRN = (DEPTH + 1) // 2
N_ATTN = DEPTH // 2
N_DENSE = (DEPTH + 1) // 2
N_MOE = DEPTH // 2

kernel_name = 'hybrid_hgrn2_dilated_moe_encoder'


def rms_norm(x, gain):
    xf = x.astype(jnp.float32)
    y = xf * lax.rsqrt(jnp.mean(xf * xf, axis=-1, keepdims=True) + EPS)
    return (y * gain.astype(jnp.float32)).astype(x.dtype)


def rope(t, pos):
    hd = t.shape[-1]
    half = hd // 2
    inv_freq = ROPE_THETA ** (-jnp.arange(half, dtype=jnp.float32) * 2.0 / hd)
    ang = pos[:, None] * inv_freq[None, :]
    cos, sin = jnp.cos(ang), jnp.sin(ang)
    t1, t2 = t[..., :half], t[..., half:]
    return jnp.concatenate([t1 * cos - t2 * sin, t2 * cos + t1 * sin], axis=-1)


def gla_chunk_scan(q, k, v, log_f):
    b, h, l, dk = q.shape
    dv = v.shape[-1]
    c = HG_CHUNK
    n = l // c

    def chunks(t):
        return jnp.moveaxis(t.reshape(b, h, n, c, t.shape[-1]), 2, 0)

    qc, kc, vc = chunks(q), chunks(k), chunks(v)
    g = jnp.cumsum(chunks(log_f), axis=-2)
    incl = jnp.tril(jnp.ones((c, c), dtype=bool))

    def step(state, inp):
        q_i, k_i, v_i, g_i = inp
        o_inter = jnp.einsum('bhtk,bhkv->bhtv', q_i * jnp.exp(g_i), state)
        diff = g_i[:, :, :, None, :] - g_i[:, :, None, :, :]
        decay = jnp.exp(jnp.where(incl[:, :, None], diff, -jnp.inf))
        a = jnp.einsum('bhtk,bhsk,bhtsk->bhts', q_i, k_i, decay)
        o_intra = jnp.einsum('bhts,bhsv->bhtv', a, v_i)
        g_last = g_i[:, :, -1:, :]
        new_state = (jnp.exp(g_last[:, :, 0, :])[..., None] * state
                     + jnp.einsum('bhsk,bhsv->bhkv', k_i * jnp.exp(g_last - g_i), v_i))
        return new_state, o_inter + o_intra

    s0 = jnp.zeros((b, h, dk, dv), jnp.float32)
    _, o = lax.scan(step, s0, (qc, kc, vc, g))
    return jnp.moveaxis(o, 0, 2).reshape(b, h, l, dv)


def hgrn2_mixer(h, w_in, lower_bound, o_gain, w_out):
    b, l, _ = h.shape
    hk = HG_HEADS * HG_DK
    hv = HG_HEADS * HG_DV
    proj = h @ w_in
    q, f_fwd, f_bwd, inp, gate = jnp.split(proj, [hk, 2 * hk, 3 * hk, 3 * hk + hv], axis=-1)

    def heads(t, dh):
        return t.astype(jnp.float32).reshape(b, l, HG_HEADS, dh).transpose(0, 2, 1, 3)

    qh = heads(jax.nn.silu(q.astype(jnp.float32)), HG_DK)
    vh = heads(inp, HG_DV)

    def direction(f_pre, lb):
        f = lb + (1.0 - lb) * jax.nn.sigmoid(f_pre.astype(jnp.float32))
        return heads(1.0 - f, HG_DK), heads(jnp.log(f), HG_DK)

    k_f, lf_f = direction(f_fwd, lower_bound[0])
    k_b, lf_b = direction(f_bwd, lower_bound[1])
    o_f = gla_chunk_scan(qh, k_f, vh, lf_f)
    flip = lambda t: jnp.flip(t, axis=2)
    o_b = flip(gla_chunk_scan(flip(qh), flip(k_b), flip(vh), flip(lf_b)))
    o = (o_f + o_b).transpose(0, 2, 1, 3)
    o = rms_norm(o, o_gain.reshape(HG_HEADS, HG_DV)).reshape(b, l, hv)
    o = o * jax.nn.silu(gate.astype(jnp.float32))
    return o.astype(h.dtype) @ w_out


def dilated_branch(q, k, v, dilation, steps):
    b, h, l, hd = q.shape
    r = dilation
    w = steps
    n = l // r

    def to_res(t):
        return t.reshape(b, h, n, r, hd).transpose(0, 1, 3, 2, 4)

    qr, kr, vr = to_res(q), to_res(k), to_res(v)
    nb = -(-n // w)
    npad = nb * w
    qb = jnp.pad(qr, ((0, 0), (0, 0), (0, 0), (0, npad - n), (0, 0))).reshape(b, h, r, nb, w, hd)

    def windows(t):
        tp = jnp.pad(t, ((0, 0), (0, 0), (0, 0), (w, npad - n + w), (0, 0))).reshape(b, h, r, nb + 2, w, hd)
        return jnp.concatenate([tp[:, :, :, 0:nb], tp[:, :, :, 1:nb + 1], tp[:, :, :, 2:nb + 2]], axis=4)

    kw, vw = windows(kr), windows(vr)
    s = jnp.einsum('bhrnqd,bhrnkd->bhrnqk', qb, kw)
    qq = jnp.arange(w)[:, None]
    kk = jnp.arange(3 * w)[None, :]
    band = (kk >= qq) & (kk <= qq + 2 * w)
    keypos = jnp.arange(nb)[:, None] * w - w + jnp.arange(3 * w)[None, :]
    inrange = (keypos >= 0) & (keypos < n)
    mask = band[None, :, :] & inrange[:, None, :]
    s = jnp.where(mask, s, NEG_INF)
    m = jnp.max(s, axis=-1, keepdims=True)
    p = jnp.exp(s - m)
    den = jnp.sum(p, axis=-1, keepdims=True)
    o = jnp.einsum('bhrnqk,bhrnkd->bhrnqd', p, vw) / den
    lse = (m + jnp.log(den))[..., 0]
    o = o.reshape(b, h, r, npad, hd)[:, :, :, :n].transpose(0, 1, 3, 2, 4).reshape(b, h, l, hd)
    lse = lse.reshape(b, h, r, npad)[:, :, :, :n].transpose(0, 1, 3, 2).reshape(b, h, l)
    return o, lse


def dilated_attention_mixer(h, w_qkv, q_gain, k_gain, w_out):
    b, l, d = h.shape
    proj = h @ w_qkv
    q, k, v = jnp.split(proj, 3, axis=-1)

    def heads(t):
        return t.astype(jnp.float32).reshape(b, l, ATT_HEADS, ATT_HEAD_DIM).transpose(0, 2, 1, 3)

    pos = jnp.arange(l, dtype=jnp.float32)
    qh = rope(rms_norm(heads(q), q_gain), pos) * (ATT_HEAD_DIM ** -0.5)
    kh = rope(rms_norm(heads(k), k_gain), pos)
    vh = heads(v)
    outs, lses = [], []
    for window, dil in DIL_BRANCHES:
        o_br, lse_br = dilated_branch(qh, kh, vh, dil, window // (2 * dil))
        outs.append(o_br)
        lses.append(lse_br)
    wts = jax.nn.softmax(jnp.stack(lses, axis=0), axis=0)
    o = jnp.sum(wts[..., None] * jnp.stack(outs, axis=0), axis=0)
    o = o.transpose(0, 2, 1, 3).reshape(b, l, d)
    return o.astype(h.dtype) @ w_out


def swiglu(h, w_gate, w_up, w_down):
    return (jax.nn.silu(h @ w_gate) * (h @ w_up)) @ w_down


def moe_swiglu(h, w_router, w_gate, w_up, w_down):
    b, l, d = h.shape
    t = b * l
    xf = h.reshape(t, d)
    logits = (xf @ w_router).astype(jnp.float32)
    top_logit, top_e = lax.top_k(logits, TOP_K)
    gates = jax.nn.softmax(top_logit, axis=-1)
    e_flat = top_e.reshape(-1).astype(jnp.int32)
    g_flat = gates.reshape(-1)
    tok_flat = jnp.repeat(jnp.arange(t, dtype=jnp.int32), TOP_K)
    order = jnp.argsort(e_flat, stable=True)
    se, stok, sg = e_flat[order], tok_flat[order], g_flat[order]
    counts = jax.ops.segment_sum(jnp.ones_like(e_flat), e_flat, num_segments=N_EXPERTS)
    padded = (counts + MOE_BLOCK - 1) // MOE_BLOCK * MOE_BLOCK
    pend = jnp.cumsum(padded)
    pstart = pend - padded
    ustart = jnp.cumsum(counts) - counts
    dest = pstart[se] + (jnp.arange(t * TOP_K, dtype=jnp.int32) - ustart[se])
    nblk = -(-(t * TOP_K) // MOE_BLOCK) + N_EXPERTS
    p = nblk * MOE_BLOCK
    slot_tok = jnp.zeros((p,), jnp.int32).at[dest].set(stok)
    slot_gate = jnp.zeros((p,), jnp.float32).at[dest].set(sg)
    blk_start = jnp.arange(nblk, dtype=pend.dtype) * MOE_BLOCK
    blk_e = jnp.minimum(jnp.searchsorted(pend, blk_start, side='right'), N_EXPERTS - 1)

    def expert_block(args):
        e, toks, g = args
        xb = xf[toks]
        y = swiglu(xb, w_gate[e], w_up[e], w_down[e])
        return y * g[:, None].astype(y.dtype)

    y = lax.map(expert_block, (blk_e, slot_tok.reshape(nblk, MOE_BLOCK), slot_gate.reshape(nblk, MOE_BLOCK)))
    out = jnp.zeros((t, d), h.dtype).at[slot_tok].add(y.reshape(p, d).astype(h.dtype))
    return out.reshape(b, l, d)


def setup_inputs(seed: int = 0) -> dict:
    key = jax.random.key(seed)
    ks = jax.random.split(key, 18)
    f32 = jnp.float32
    hk = HG_HEADS * HG_DK
    hv = HG_HEADS * HG_DV

    def nrm(k, shape, fan_in):
        return jax.random.normal(k, shape, f32) * (fan_in ** -0.5)

    def gain(k, shape):
        return 1.0 + 0.02 * jax.random.normal(k, shape, f32)

    return {
        'x': jax.random.normal(ks[0], (BATCH, SEQ, D_MODEL), f32),
        'norm_gains': gain(ks[1], (DEPTH, 2, D_MODEL)),
        'hgrn_w_in': nrm(ks[2], (N_HGRN, D_MODEL, 3 * hk + 2 * hv), D_MODEL),
        'hgrn_lb_logits': 0.5 * jax.random.normal(ks[3], (DEPTH + 1, 2, hk), f32),
        'hgrn_onorm': gain(ks[4], (N_HGRN, hv)),
        'hgrn_w_out': nrm(ks[5], (N_HGRN, hv, D_MODEL), hv),
        'attn_w_qkv': nrm(ks[6], (N_ATTN, D_MODEL, 3 * D_MODEL), D_MODEL),
        'attn_q_gain': gain(ks[7], (N_ATTN, ATT_HEAD_DIM)),
        'attn_k_gain': gain(ks[8], (N_ATTN, ATT_HEAD_DIM)),
        'attn_w_out': nrm(ks[9], (N_ATTN, D_MODEL, D_MODEL), D_MODEL),
        'ffn_w_gate': nrm(ks[10], (N_DENSE, D_MODEL, FFN_DENSE), D_MODEL),
        'ffn_w_up': nrm(ks[11], (N_DENSE, D_MODEL, FFN_DENSE), D_MODEL),
        'ffn_w_down': nrm(ks[12], (N_DENSE, FFN_DENSE, D_MODEL), FFN_DENSE),
        'moe_w_router': nrm(ks[13], (N_MOE, D_MODEL, N_EXPERTS), D_MODEL),
        'moe_w_gate': nrm(ks[14], (N_MOE, N_EXPERTS, D_MODEL, FFN_EXPERT), D_MODEL),
        'moe_w_up': nrm(ks[15], (N_MOE, N_EXPERTS, D_MODEL, FFN_EXPERT), D_MODEL),
        'moe_w_down': nrm(ks[16], (N_MOE, N_EXPERTS, FFN_EXPERT, D_MODEL), FFN_EXPERT),
    }


def reference(x, norm_gains, hgrn_w_in, hgrn_lb_logits, hgrn_onorm, hgrn_w_out,
              attn_w_qkv, attn_q_gain, attn_k_gain, attn_w_out,
              ffn_w_gate, ffn_w_up, ffn_w_down,
              moe_w_router, moe_w_gate, moe_w_up, moe_w_down):
    lb_all = jnp.cumsum(jax.nn.softmax(hgrn_lb_logits.astype(jnp.float32), axis=0), axis=0)
    for i in range(DEPTH):
        j = i // N_MIXERS
        hn = rms_norm(x, norm_gains[i, 0])
        if i % N_MIXERS == 0:
            mix = hgrn2_mixer(hn, hgrn_w_in[j], lb_all[i], hgrn_onorm[j], hgrn_w_out[j])
        else:
            mix = dilated_attention_mixer(hn, attn_w_qkv[j], attn_q_gain[j], attn_k_gain[j], attn_w_out[j])
        x = x + mix.astype(x.dtype)
        hn = rms_norm(x, norm_gains[i, 1])
        jf = i // 2
        if i % 2 == 0:
            ff = swiglu(hn, ffn_w_gate[jf], ffn_w_up[jf], ffn_w_down[jf])
        else:
            ff = moe_swiglu(hn, moe_w_router[jf], moe_w_gate[jf], moe_w_up[jf], moe_w_down[jf])
        x = x + ff.astype(x.dtype)
    return x
```

```python
import functools

import jax
import jax.numpy as jnp
from jax import lax
from jax.experimental import pallas as pl
from jax.experimental.pallas import tpu as pltpu

F32 = jnp.float32
BF16 = jnp.bfloat16
EPS = 1e-6
NEG_INF = -1e30
ROPE_THETA = 10000.0

HEAD_DIM = 128
GLA_CHUNK = 64
DIL_BRANCHES = ((128, 1), (512, 4), (2048, 16))
N_EXPERTS = 8
TOP_K = 2

LANES = 128
V7X_VMEM_BYTES = 64 * 1024 * 1024
VMEM_BUDGET = 56 * 1024 * 1024

_NT = (((1,), (1,)), ((), ()))
_TN = (((0,), (0,)), ((), ()))


def _params(semantics, vmem_bytes):
    return pltpu.CompilerParams(dimension_semantics=semantics,
                                vmem_limit_bytes=int(min(vmem_bytes, VMEM_BUDGET)))


def _silu(x):
    return x * jax.nn.sigmoid(x)


def _rms_rows(x, gain):
    ms = jnp.mean(x * x, axis=-1, keepdims=True)
    return x * lax.rsqrt(ms + EPS) * gain


def _norm_matmul_body(x_ref, g_ref, w_ref, *rest, n_aux, epilogue):
    aux, outs, hn_ref = rest[:n_aux], rest[n_aux:-1], rest[-1]
    j = pl.program_id(1)

    @pl.when(j == 0)
    def _():
        hn_ref[...] = _rms_rows(x_ref[...], g_ref[...]).astype(BF16)

    acc = jnp.dot(hn_ref[...], w_ref[...], preferred_element_type=F32)
    epilogue(acc, j, aux, outs)


def norm_matmul(name, x, gain, w, epilogue, out_shapes, out_specs, aux=(), aux_specs=(), tm=1024, tn=512):
    m, d = x.shape
    n = w.shape[1]
    assert m % tm == 0 and n % tn == 0
    out_bytes = sum(2 * tm * tn * jnp.dtype(s.dtype).itemsize for s in out_shapes)
    vmem = 2 * tm * d * 4 + tm * d * 2 + 2 * d * tn * 2 + out_bytes + 4 * tm * tn * 4 + (4 << 20)
    return pl.pallas_call(
        functools.partial(_norm_matmul_body, n_aux=len(aux), epilogue=epilogue),
        grid=(m // tm, n // tn),
        in_specs=[pl.BlockSpec((tm, d), lambda i, j: (i, 0)),
                  pl.BlockSpec((1, d), lambda i, j: (0, 0)),
                  pl.BlockSpec((d, tn), lambda i, j: (0, j)),
                  *aux_specs],
        out_specs=out_specs,
        out_shape=out_shapes,
        scratch_shapes=[pltpu.VMEM((tm, d), BF16)],
        compiler_params=_params(("parallel", "arbitrary"), vmem),
        name=name,
    )(x, gain.reshape(1, d), w, *aux)


def _group_row_bcast(g, group, idx):
    rows, lanes = g.shape
    if group >= 8:
        g3 = g.reshape(rows // group, group, lanes)
        return jnp.broadcast_to(g3[:, idx:idx + 1, :], g3.shape).reshape(rows, lanes)
    pos = lax.broadcasted_iota(jnp.int32, g.shape, 0) & (group - 1)
    r = g
    for m in range(group):
        off = idx - m
        if off != 0:
            r = jnp.where(pos == m, pltpu.roll(g, (-off) % rows, 0), r)
    return r


def _gla_body(q_ref, k_ref, v_ref, lf_ref, o_ref, st_ref, *, fwd, chunk):
    rows = q_ref.shape[0]
    nch = rows // chunk
    c = chunk

    @pl.when(pl.program_id(2) == 0)
    def _():
        st_ref[...] = jnp.zeros_like(st_ref)

    ti = lax.broadcasted_iota(jnp.int32, (c, c), 0)
    si = lax.broadcasted_iota(jnp.int32, (c, c), 1)
    tri = ((si <= ti) if fwd else (si >= ti)).astype(F32)
    diag = ti == si
    levels = []
    h = c // 2
    while h >= 1:
        same = (ti & -(2 * h)) == (si & -(2 * h))
        t_hi = (ti & h) != 0
        s_hi = (si & h) != 0
        levels.append((h, same & (t_hi & ~s_hi if fwd else ~t_hi & s_hi)))
        h //= 2

    def one_chunk(ci, carry):
        cc = ci if fwd else nch - 1 - ci
        r0 = pl.multiple_of(cc * c, c)
        q = q_ref[pl.ds(r0, c), :]
        k = k_ref[pl.ds(r0, c), :]
        v = v_ref[pl.ds(r0, c), :]
        lf = lf_ref[pl.ds(r0, c), :]
        qf = q.astype(F32)
        kf = k.astype(F32)
        g = jnp.dot(tri, lf, precision=lax.Precision.HIGHEST, preferred_element_type=F32)
        g_tot = g[c - 1:c, :] if fwd else g[0:1, :]
        st = st_ref[...]
        qd = (qf * jnp.exp(g)).astype(BF16)
        o = lax.dot_general(qd, st.astype(BF16), _NT, preferred_element_type=F32)
        a = jnp.where(diag, lax.dot_general(q, k, _NT, preferred_element_type=F32), 0.0)
        for half, mask in levels:
            r = _group_row_bcast(g, 2 * half, half - 1 if fwd else half)
            qe = (qf * jnp.exp(jnp.minimum(g - r, 0.0))).astype(BF16)
            ke = (kf * jnp.exp(jnp.minimum(r - g, 0.0))).astype(BF16)
            a = a + jnp.where(mask, lax.dot_general(qe, ke, _NT, preferred_element_type=F32), 0.0)
        o = o + jnp.dot(a.astype(BF16), v, preferred_element_type=F32)
        o_ref[pl.ds(r0, c), :] = o.astype(o_ref.dtype)
        kd = (kf * jnp.exp(g_tot - g)).astype(BF16)
        st_ref[...] = st * jnp.exp(g_tot) + lax.dot_general(v, kd, _TN, preferred_element_type=F32)
        return carry

    lax.fori_loop(0, nch, one_chunk, 0)


def gla_direction(qvg, kk, logf, *, batch, seq, heads, fwd, rows_per_step=512):
    t = batch * seq
    hd = HEAD_DIM
    rb = rows_per_step
    ns = seq // rb
    d = 0 if fwd else 1

    def rowblk(b, s):
        return b * ns + (s if fwd else ns - 1 - s)

    return pl.pallas_call(
        functools.partial(_gla_body, fwd=fwd, chunk=GLA_CHUNK),
        grid=(batch, heads, ns),
        in_specs=[pl.BlockSpec((rb, hd), lambda b, h, s: (rowblk(b, s), h)),
                  pl.BlockSpec((rb, hd), lambda b, h, s: (rowblk(b, s), d * heads + h)),
                  pl.BlockSpec((rb, hd), lambda b, h, s: (rowblk(b, s), heads + h)),
                  pl.BlockSpec((rb, hd), lambda b, h, s: (rowblk(b, s), d * heads + h))],
        out_specs=pl.BlockSpec((rb, hd), lambda b, h, s: (rowblk(b, s), h)),
        out_shape=jax.ShapeDtypeStruct((t, heads * hd), BF16),
        scratch_shapes=[pltpu.VMEM((hd, hd), F32)],
        compiler_params=_params(("parallel", "parallel", "arbitrary"), 16 << 20),
        name="gla_fwd" if fwd else "gla_bwd",
    )(qvg, kk, qvg, logf)


def _proj_residual_body(*refs, n_pro, prologue):
    pro, (w_ref, x_ref, o_ref, y_ref) = refs[:n_pro], refs[n_pro:]

    @pl.when(pl.program_id(1) == 0)
    def _():
        prologue(pro, y_ref)

    o_ref[...] = x_ref[...] + jnp.dot(y_ref[...], w_ref[...], preferred_element_type=F32)


def proj_residual(name, pro_inputs, pro_specs, prologue, w, xres, tm=512, tn=512):
    m, n = xres.shape
    kdim = w.shape[0]
    pro_bytes = sum(2 * tm * kdim * jnp.dtype(a.dtype).itemsize for a in pro_inputs)
    vmem = pro_bytes + tm * kdim * 2 + 2 * kdim * tn * 2 + 4 * tm * tn * 4 + 4 * tm * kdim * 4 + (4 << 20)
    return pl.pallas_call(
        functools.partial(_proj_residual_body, n_pro=len(pro_inputs), prologue=prologue),
        grid=(m // tm, n // tn),
        in_specs=[*pro_specs,
                  pl.BlockSpec((kdim, tn), lambda i, j: (0, j)),
                  pl.BlockSpec((tm, tn), lambda i, j: (i, j))],
        out_specs=pl.BlockSpec((tm, tn), lambda i, j: (i, j)),
        out_shape=jax.ShapeDtypeStruct((m, n), F32),
        scratch_shapes=[pltpu.VMEM((tm, kdim), BF16)],
        compiler_params=_params(("parallel", "arbitrary"), vmem),
        name=name,
    )(*pro_inputs, w, xres)


def _hgrn_out_prologue(pro, y_ref):
    of_ref, ob_ref, gate_ref, gain_ref = pro
    for h in range(of_ref.shape[1] // HEAD_DIM):
        hs = slice(h * HEAD_DIM, (h + 1) * HEAD_DIM)
        o = of_ref[:, hs].astype(F32) + ob_ref[:, hs].astype(F32)
        y = _rms_rows(o, gain_ref[:, hs]) * gate_ref[:, hs].astype(F32)
        y_ref[:, hs] = y.astype(BF16)


def _attn_merge_prologue(pro, y_ref):
    o_refs, l_refs = pro[:3], pro[3:]
    for h in range(y_ref.shape[1] // HEAD_DIM):
        hs = slice(h * HEAD_DIM, (h + 1) * HEAD_DIM)
        ls = [r[:, hs] for r in l_refs]
        m = jnp.maximum(jnp.maximum(ls[0], ls[1]), ls[2])
        es = [jnp.exp(l - m) for l in ls]
        den = es[0] + es[1] + es[2]
        o = sum((e / den) * r[:, hs].astype(F32) for e, r in zip(es, o_refs))
        y_ref[:, hs] = o.astype(BF16)


def _swiglu_body(x_ref, g_ref, wg_ref, wu_ref, wd_ref, o_ref, hn_ref):
    @pl.when(pl.program_id(1) == 0)
    def _():
        x = x_ref[...]
        hn_ref[...] = _rms_rows(x, g_ref[...]).astype(BF16)
        o_ref[...] = x

    hn = hn_ref[...]
    a = jnp.dot(hn, wg_ref[...], preferred_element_type=F32)
    u = jnp.dot(hn, wu_ref[...], preferred_element_type=F32)
    h = (_silu(a) * u).astype(BF16)
    o_ref[...] += jnp.dot(h, wd_ref[...], preferred_element_type=F32)


def norm_swiglu_residual(x, gain, wg, wu, wd, tm=512, tf=512):
    m, d = x.shape
    f = wg.shape[1]
    assert m % tm == 0 and f % tf == 0
    vmem = 4 * tm * d * 4 + tm * d * 2 + 2 * (2 * d * tf + tf * d) * 2 + 3 * tm * tf * 4 + 2 * tm * d * 4 + (4 << 20)
    return pl.pallas_call(
        _swiglu_body,
        grid=(m // tm, f // tf),
        in_specs=[pl.BlockSpec((tm, d), lambda i, j: (i, 0)),
                  pl.BlockSpec((1, d), lambda i, j: (0, 0)),
                  pl.BlockSpec((d, tf), lambda i, j: (0, j)),
                  pl.BlockSpec((d, tf), lambda i, j: (0, j)),
                  pl.BlockSpec((tf, d), lambda i, j: (j, 0))],
        out_specs=pl.BlockSpec((tm, d), lambda i, j: (i, 0)),
        out_shape=jax.ShapeDtypeStruct((m, d), F32),
        scratch_shapes=[pltpu.VMEM((tm, d), BF16)],
        compiler_params=_params(("parallel", "arbitrary"), vmem),
        name="dense_swiglu",
    )(x, gain.reshape(1, d), wg, wu, wd)


def _band_attn_body(q_ref, k_ref, v_ref, o_ref, l_ref, *, steps, tq):
    n, width = q_ref.shape
    wk = min(n, tq + 2 * steps)
    nqb = n // tq

    def one_block(qi, carry):
        q0 = pl.multiple_of(qi * tq, tq)
        k0 = pl.multiple_of(jnp.clip(q0 - steps, 0, n - wk), steps)
        qpos = q0 + lax.broadcasted_iota(jnp.int32, (tq, wk), 0)
        kpos = k0 + lax.broadcasted_iota(jnp.int32, (tq, wk), 1)
        valid = jnp.abs(kpos - qpos) <= steps
        for h in range(width // HEAD_DIM):
            hs = slice(h * HEAD_DIM, (h + 1) * HEAD_DIM)
            q = q_ref[pl.ds(q0, tq), hs]
            kw = k_ref[pl.ds(k0, wk), hs]
            vw = v_ref[pl.ds(k0, wk), hs]
            s = lax.dot_general(q, kw, _NT, preferred_element_type=F32)
            s = jnp.where(valid, s, NEG_INF)
            m = jnp.max(s, axis=-1, keepdims=True)
            p = jnp.exp(s - m)
            den = jnp.sum(p, axis=-1, keepdims=True)
            o = jnp.dot(p.astype(BF16), vw, preferred_element_type=F32) / den
            o_ref[pl.ds(q0, tq), hs] = o.astype(o_ref.dtype)
            l_ref[pl.ds(q0, tq), hs] = jnp.broadcast_to(m + jnp.log(den), (tq, HEAD_DIM))
        return carry

    lax.fori_loop(0, nqb, one_block, 0)


def band_attention(qkv, *, batch, seq, heads, dilation, steps, heads_per_step, tq=256):
    t = batch * seq
    r = dilation
    n = seq // r
    d = heads * HEAD_DIM
    hg = heads_per_step
    width = hg * HEAD_DIM
    view = qkv.reshape(t // r, r * 3 * d)
    per_class = 3 * d // width
    per_part = d // width
    tq = min(tq, n)

    def col(part):
        return lambda b, c, g: (b, c * per_class + part * per_part + g)

    out_col = lambda b, c, g: (b, c * per_part + g)
    o, lse = pl.pallas_call(
        functools.partial(_band_attn_body, steps=steps, tq=tq),
        grid=(batch, r, heads // hg),
        in_specs=[pl.BlockSpec((n, width), col(0)),
                  pl.BlockSpec((n, width), col(1)),
                  pl.BlockSpec((n, width), col(2))],
        out_specs=[pl.BlockSpec((n, width), out_col), pl.BlockSpec((n, width), out_col)],
        out_shape=[jax.ShapeDtypeStruct((t // r, r * d), BF16),
                   jax.ShapeDtypeStruct((t // r, r * d), F32)],
        compiler_params=_params(("parallel", "parallel", "parallel"),
                                2 * n * width * (3 * 2 + 2 + 4) + (8 << 20)),
        name=f"band_attn_r{r}",
    )(view, view, view)
    return o.reshape(t, d), lse.reshape(t, d)


def _router_body(x_ref, g_ref, wr_ref, hn_ref, e_ref, p_ref):
    hn = _rms_rows(x_ref[...], g_ref[...])
    hn_ref[...] = hn
    logits = jnp.dot(hn, wr_ref[...], precision=lax.Precision.HIGHEST, preferred_element_type=F32)
    lane = lax.broadcasted_iota(jnp.int32, logits.shape, 1)
    logits = jnp.where(lane < N_EXPERTS, logits, -jnp.inf)
    m1 = jnp.max(logits, axis=-1, keepdims=True)
    i1 = jnp.min(jnp.where(logits == m1, lane, LANES), axis=-1, keepdims=True)
    rest = jnp.where(lane == i1, -jnp.inf, logits)
    m2 = jnp.max(rest, axis=-1, keepdims=True)
    i2 = jnp.min(jnp.where(rest == m2, lane, LANES), axis=-1, keepdims=True)
    e2 = jnp.exp(m2 - m1)
    den = 1.0 + e2
    e_ref[...] = jnp.where(lane == 0, i1, jnp.where(lane == 1, i2, 0))
    p_ref[...] = jnp.where(lane == 0, 1.0 / den, jnp.where(lane == 1, e2 / den, 0.0))


def moe_router(x, gain, w_router_padded, tm=512):
    m, d = x.shape
    return pl.pallas_call(
        _router_body,
        grid=(m // tm,),
        in_specs=[pl.BlockSpec((tm, d), lambda i: (i, 0)),
                  pl.BlockSpec((1, d), lambda i: (0, 0)),
                  pl.BlockSpec((d, LANES), lambda i: (0, 0))],
        out_specs=[pl.BlockSpec((tm, d), lambda i: (i, 0)),
                   pl.BlockSpec((tm, LANES), lambda i: (i, 0)),
                   pl.BlockSpec((tm, LANES), lambda i: (i, 0))],
        out_shape=[jax.ShapeDtypeStruct((m, d), F32),
                   jax.ShapeDtypeStruct((m, LANES), jnp.int32),
                   jax.ShapeDtypeStruct((m, LANES), F32)],
        compiler_params=_params(("parallel",), 6 * tm * d * 4 + (8 << 20)),
        name="moe_router",
    )(x, gain.reshape(1, d), w_router_padded)


def _expert_body(blk_e_ref, n_used_ref, tok_ref, gate_ref, hn_hbm, wg_ref, wu_ref, wd_ref,
                 y_ref, xf_ref, xb_ref, sem):
    b = pl.program_id(0)
    f = pl.program_id(1)
    nf = pl.num_programs(1)
    tb = y_ref.shape[0]
    used = b < n_used_ref[0]

    def row_copy(i):
        return pltpu.make_async_copy(hn_hbm.at[pl.ds(tok_ref[0, i], 1), :],
                                     xf_ref.at[pl.ds(i, 1), :], sem)

    @pl.when(used & (f == 0))
    def _():
        def start(i, c):
            row_copy(i).start()
            return c

        def wait(i, c):
            row_copy(i).wait()
            return c

        lax.fori_loop(0, tb, start, 0)
        lax.fori_loop(0, tb, wait, 0)
        xb_ref[...] = xf_ref[...].astype(BF16)
        y_ref[...] = jnp.zeros_like(y_ref)

    @pl.when(~used & (f == 0))
    def _():
        y_ref[...] = jnp.zeros_like(y_ref)

    @pl.when(used)
    def _():
        xb = xb_ref[...]
        a = jnp.dot(xb, wg_ref[...], preferred_element_type=F32)
        u = jnp.dot(xb, wu_ref[...], preferred_element_type=F32)
        h = (_silu(a) * u).astype(BF16)
        y_ref[...] += jnp.dot(h, wd_ref[...], preferred_element_type=F32)

    @pl.when(used & (f == nf - 1))
    def _():
        y_ref[...] = y_ref[...] * gate_ref[...]


def moe_experts(hn, slot_tok, slot_gate, blk_e, n_used, wg, wu, wd, tb, tf=512):
    t, d = hn.shape
    p = slot_tok.shape[0]
    nblk = p // tb
    fdim = wg.shape[2]
    nf = fdim // tf

    def live(b, n_used_ref):
        return jnp.minimum(b, n_used_ref[0] - 1)

    def wmap_cols(b, f, blk_e_ref, n_used_ref):
        return (blk_e_ref[live(b, n_used_ref)], 0, jnp.where(b < n_used_ref[0], f, nf - 1))

    def wmap_rows(b, f, blk_e_ref, n_used_ref):
        return (blk_e_ref[live(b, n_used_ref)], jnp.where(b < n_used_ref[0], f, nf - 1), 0)

    vmem = (tb * d * (4 + 2) + 2 * tb * d * 4 + 2 * (2 * d * tf + tf * d) * 2 + 3 * tb * tf * 4
            + 2 * tb * d * 4 + (4 << 20))
    grid_spec = pltpu.PrefetchScalarGridSpec(
        num_scalar_prefetch=2,
        grid=(nblk, nf),
        in_specs=[pl.BlockSpec((None, 1, tb), lambda b, f, *_: (b, 0, 0), memory_space=pltpu.SMEM),
                  pl.BlockSpec((tb, 1), lambda b, f, *_: (b, 0)),
                  pl.BlockSpec(memory_space=pl.ANY),
                  pl.BlockSpec((None, d, tf), wmap_cols),
                  pl.BlockSpec((None, d, tf), wmap_cols),
                  pl.BlockSpec((None, tf, d), wmap_rows)],
        out_specs=pl.BlockSpec((tb, d), lambda b, f, *_: (b, 0)),
        scratch_shapes=[pltpu.VMEM((tb, d), F32), pltpu.VMEM((tb, d), BF16), pltpu.SemaphoreType.DMA(())],
    )
    return pl.pallas_call(
        _expert_body,
        grid_spec=grid_spec,
        out_shape=jax.ShapeDtypeStruct((p, d), F32),
        compiler_params=_params(("arbitrary", "arbitrary"), vmem),
        name="moe_experts",
    )(blk_e, n_used, slot_tok.reshape(nblk, 1, tb), slot_gate.reshape(p, 1), hn, wg, wu, wd)


def _combine_body(slot_ref, x_ref, y_hbm, o_ref, buf_ref, sem):
    tm = x_ref.shape[0]

    def row_copy(i):
        k = i // tm
        r = i - k * tm
        return pltpu.make_async_copy(y_hbm.at[pl.ds(slot_ref[0, i], 1), :],
                                     buf_ref.at[k, pl.ds(r, 1), :], sem)

    def start(i, c):
        row_copy(i).start()
        return c

    def wait(i, c):
        row_copy(i).wait()
        return c

    lax.fori_loop(0, TOP_K * tm, start, 0)
    lax.fori_loop(0, TOP_K * tm, wait, 0)
    o_ref[...] = x_ref[...] + (buf_ref[0] + buf_ref[1])


def moe_combine(x, y, tok_slots, tm=256):
    t, d = x.shape
    nt = t // tm
    slots = tok_slots.reshape(nt, tm, TOP_K).transpose(0, 2, 1).reshape(nt, 1, TOP_K * tm)
    return pl.pallas_call(
        _combine_body,
        grid=(nt,),
        in_specs=[pl.BlockSpec((None, 1, TOP_K * tm), lambda i: (i, 0, 0), memory_space=pltpu.SMEM),
                  pl.BlockSpec((tm, d), lambda i: (i, 0)),
                  pl.BlockSpec(memory_space=pl.ANY)],
        out_specs=pl.BlockSpec((tm, d), lambda i: (i, 0)),
        out_shape=jax.ShapeDtypeStruct((t, d), F32),
        scratch_shapes=[pltpu.VMEM((TOP_K, tm, d), F32), pltpu.SemaphoreType.DMA(())],
        compiler_params=_params(("arbitrary",), (4 + TOP_K) * tm * d * 4 + (4 << 20)),
        name="moe_combine",
    )(slots, x, y)


def _routing_plan(top_e, gates, tb):
    t = top_e.shape[0]
    n = t * TOP_K
    e_flat = top_e.reshape(n)
    onehot = (e_flat[:, None] == jnp.arange(N_EXPERTS, dtype=jnp.int32)[None, :]).astype(jnp.int32)
    rank = jnp.take_along_axis(jnp.cumsum(onehot, axis=0) - onehot, e_flat[:, None], axis=1)[:, 0]
    counts = jnp.sum(onehot, axis=0)
    padded = (counts + tb - 1) // tb * tb
    pend = jnp.cumsum(padded)
    dest = (pend - padded)[e_flat] + rank
    nblk = -(-n // tb) + N_EXPERTS
    p = nblk * tb
    tok_flat = jnp.arange(n, dtype=jnp.int32) // TOP_K
    slot_tok = jnp.zeros((p,), jnp.int32).at[dest].set(tok_flat)
    slot_gate = jnp.zeros((p,), F32).at[dest].set(gates.reshape(n))
    blk_start = jnp.arange(nblk, dtype=pend.dtype) * tb
    blk_e = jnp.minimum(jnp.searchsorted(pend, blk_start, side='right'), N_EXPERTS - 1).astype(jnp.int32)
    n_used = (pend[-1] // tb).astype(jnp.int32).reshape(1)
    return slot_tok, slot_gate, blk_e, n_used, dest.reshape(t, TOP_K).astype(jnp.int32)


def _rope_tables(seq, hd):
    half = hd // 2
    inv_freq = ROPE_THETA ** (-jnp.arange(half, dtype=F32) * 2.0 / hd)
    ang = jnp.arange(seq, dtype=F32)[:, None] * inv_freq[None, :]
    cos, sin = jnp.cos(ang), jnp.sin(ang)
    return jnp.concatenate([cos, cos], axis=-1), jnp.concatenate([-sin, sin], axis=-1)


def kernel(x, norm_gains, hgrn_w_in, hgrn_lb_logits, hgrn_onorm, hgrn_w_out, attn_w_qkv, attn_q_gain,
           attn_k_gain, attn_w_out, ffn_w_gate, ffn_w_up, ffn_w_down, moe_w_router, moe_w_gate, moe_w_up,
           moe_w_down):
    batch, seq, d = x.shape
    t = batch * seq
    heads = d // HEAD_DIM
    hd = HEAD_DIM
    xf = x.reshape(t, d)

    w_in = hgrn_w_in[0]
    w_qvg = jnp.concatenate([w_in[:, :d], w_in[:, 3 * d:5 * d]], axis=1).astype(BF16)
    w_f = w_in[:, d:3 * d].astype(BF16)
    lb = jnp.cumsum(jax.nn.softmax(hgrn_lb_logits.astype(F32), axis=0), axis=0)[0].reshape(1, 2 * d)
    tn = 512
    nq = d // tn

    def qvg_epilogue(acc, j, aux, outs):
        is_v = (j >= nq) & (j < 2 * nq)
        outs[0][...] = jnp.where(is_v, acc, _silu(acc)).astype(BF16)

    (qvg,) = norm_matmul(
        "hgrn_qvg_proj", xf, norm_gains[0, 0], w_qvg, qvg_epilogue,
        [jax.ShapeDtypeStruct((t, 3 * d), BF16)], [pl.BlockSpec((1024, tn), lambda i, j: (i, j))], tn=tn)

    def f_epilogue(acc, j, aux, outs):
        lbv = aux[0][...]
        fgate = lbv + (1.0 - lbv) * jax.nn.sigmoid(acc)
        outs[0][...] = jnp.log(fgate)
        outs[1][...] = (1.0 - fgate).astype(BF16)

    logf, kk = norm_matmul(
        "hgrn_forget_proj", xf, norm_gains[0, 0], w_f, f_epilogue,
        [jax.ShapeDtypeStruct((t, 2 * d), F32), jax.ShapeDtypeStruct((t, 2 * d), BF16)],
        [pl.BlockSpec((1024, tn), lambda i, j: (i, j)), pl.BlockSpec((1024, tn), lambda i, j: (i, j))],
        aux=(lb,), aux_specs=(pl.BlockSpec((1, tn), lambda i, j: (0, j)),), tn=tn)

    o_f = gla_direction(qvg, kk, logf, batch=batch, seq=seq, heads=heads, fwd=True)
    o_b = gla_direction(qvg, kk, logf, batch=batch, seq=seq, heads=heads, fwd=False)

    tm = 512
    x1 = proj_residual(
        "hgrn_out_proj", (o_f, o_b, qvg, hgrn_onorm[0].reshape(1, d)),
        (pl.BlockSpec((tm, d), lambda i, j: (i, 0)), pl.BlockSpec((tm, d), lambda i, j: (i, 0)),
         pl.BlockSpec((tm, d), lambda i, j: (i, 2)), pl.BlockSpec((1, d), lambda i, j: (0, 0))),
        _hgrn_out_prologue, hgrn_w_out[0].astype(BF16), xf, tm=tm)

    fdim = ffn_w_gate.shape[2]
    fpad = -(-fdim // 512) * 512 - fdim
    wg = jnp.pad(ffn_w_gate[0], ((0, 0), (0, fpad))).astype(BF16)
    wu = jnp.pad(ffn_w_up[0], ((0, 0), (0, fpad))).astype(BF16)
    wd = jnp.pad(ffn_w_down[0], ((0, fpad), (0, 0))).astype(BF16)
    x2 = norm_swiglu_residual(x1, norm_gains[0, 1], wg, wu, wd)

    cos, sin = _rope_tables(seq, hd)
    qg = attn_q_gain[0].reshape(1, hd)
    kg = attn_k_gain[0].reshape(1, hd)
    tm_qkv = 1024
    pos_blocks = seq // tm_qkv

    def qkv_epilogue(acc, j, aux, outs):
        cos_ref, sin_ref, qg_ref, kg_ref = aux
        o_ref = outs[0]

        def normed_rope(gain, scale):
            for h in range(tn // hd):
                hs = slice(h * hd, (h + 1) * hd)
                y = _rms_rows(acc[:, hs], gain)
                y = y * cos_ref[...] + pltpu.roll(y, hd // 2, 1) * sin_ref[...]
                o_ref[:, hs] = (y * scale).astype(BF16)

        @pl.when(j < nq)
        def _():
            normed_rope(qg_ref[...], hd ** -0.5)

        @pl.when((j >= nq) & (j < 2 * nq))
        def _():
            normed_rope(kg_ref[...], 1.0)

        @pl.when(j >= 2 * nq)
        def _():
            o_ref[...] = acc.astype(BF16)

    (qkv,) = norm_matmul(
        "attn_qkv_proj", x2, norm_gains[1, 0], attn_w_qkv[0].astype(BF16), qkv_epilogue,
        [jax.ShapeDtypeStruct((t, 3 * d), BF16)], [pl.BlockSpec((tm_qkv, tn), lambda i, j: (i, j))],
        aux=(cos, sin, qg, kg),
        aux_specs=(pl.BlockSpec((tm_qkv, hd), lambda i, j: (i % pos_blocks, 0)),
                   pl.BlockSpec((tm_qkv, hd), lambda i, j: (i % pos_blocks, 0)),
                   pl.BlockSpec((1, hd), lambda i, j: (0, 0)),
                   pl.BlockSpec((1, hd), lambda i, j: (0, 0))),
        tm=tm_qkv, tn=tn)

    branch_o, branch_l = [], []
    for (window, dil), hg in zip(DIL_BRANCHES, (1, 2, 4)):
        o_br, l_br = band_attention(qkv, batch=batch, seq=seq, heads=heads, dilation=dil,
                                    steps=window // (2 * dil), heads_per_step=hg)
        branch_o.append(o_br)
        branch_l.append(l_br)

    tm_merge = 256
    row_spec = pl.BlockSpec((tm_merge, d), lambda i, j: (i, 0))
    x3 = proj_residual("attn_out_proj", (*branch_o, *branch_l), (row_spec,) * 6, _attn_merge_prologue,
                       attn_w_out[0].astype(BF16), x2, tm=tm_merge)

    tb = 512
    w_router = jnp.pad(moe_w_router[0].astype(F32), ((0, 0), (0, LANES - N_EXPERTS)))
    hn3, e_pad, p_pad = moe_router(x3, norm_gains[1, 1], w_router)
    slot_tok, slot_gate, blk_e, n_used, tok_slots = _routing_plan(e_pad[:, :TOP_K], p_pad[:, :TOP_K], tb)
    y = moe_experts(hn3, slot_tok, slot_gate, blk_e, n_used, moe_w_gate[0].astype(BF16),
                    moe_w_up[0].astype(BF16), moe_w_down[0].astype(BF16), tb)
    out = moe_combine(x3, y, tok_slots)
    return out.reshape(batch, seq, d)
```

```python
import functools

import jax
import jax.numpy as jnp
from jax import lax
from jax.experimental import pallas as pl
from jax.experimental.pallas import tpu as pltpu

F32 = jnp.float32
BF16 = jnp.bfloat16
EPS = 1e-6
NEG_INF = -1e30
ROPE_THETA = 10000.0

HEAD_DIM = 128
GLA_CHUNK = 64
DIL_BRANCHES = ((128, 1), (512, 4), (2048, 16))
N_EXPERTS = 8
TOP_K = 2

LANES = 128
V7X_VMEM_BYTES = 64 * 1024 * 1024
VMEM_BUDGET = 56 * 1024 * 1024

_NT = (((1,), (1,)), ((), ()))
_TN = (((0,), (0,)), ((), ()))


def _params(semantics, vmem_bytes):
    return pltpu.CompilerParams(dimension_semantics=semantics,
                                vmem_limit_bytes=int(min(vmem_bytes, VMEM_BUDGET)))


def _silu(x):
    return x * jax.nn.sigmoid(x)


def _rms_rows(x, gain):
    ms = jnp.mean(x * x, axis=-1, keepdims=True)
    return x * lax.rsqrt(ms + EPS) * gain


def _norm_matmul_body(x_ref, g_ref, w_ref, *rest, n_aux, epilogue):
    aux, outs, hn_ref = rest[:n_aux], rest[n_aux:-1], rest[-1]
    j = pl.program_id(1)

    @pl.when(j == 0)
    def _():
        hn_ref[...] = _rms_rows(x_ref[...], g_ref[...]).astype(BF16)

    acc = jnp.dot(hn_ref[...], w_ref[...], preferred_element_type=F32)
    epilogue(acc, j, aux, outs)


def norm_matmul(name, x, gain, w, epilogue, out_shapes, out_specs, aux=(), aux_specs=(), tm=1024, tn=512):
    m, d = x.shape
    n = w.shape[1]
    assert m % tm == 0 and n % tn == 0
    out_bytes = sum(2 * tm * tn * jnp.dtype(s.dtype).itemsize for s in out_shapes)
    vmem = 2 * tm * d * 4 + tm * d * 2 + 2 * d * tn * 2 + out_bytes + 4 * tm * tn * 4 + (4 << 20)
    return pl.pallas_call(
        functools.partial(_norm_matmul_body, n_aux=len(aux), epilogue=epilogue),
        grid=(m // tm, n // tn),
        in_specs=[pl.BlockSpec((tm, d), lambda i, j: (i, 0)),
                  pl.BlockSpec((1, d), lambda i, j: (0, 0)),
                  pl.BlockSpec((d, tn), lambda i, j: (0, j)),
                  *aux_specs],
        out_specs=out_specs,
        out_shape=out_shapes,
        scratch_shapes=[pltpu.VMEM((tm, d), BF16)],
        compiler_params=_params(("parallel", "arbitrary"), vmem),
        name=name,
    )(x, gain.reshape(1, d), w, *aux)


def _group_row_bcast(g, group, idx):
    rows, lanes = g.shape
    if group >= 8:
        g3 = g.reshape(rows // group, group, lanes)
        return jnp.broadcast_to(g3[:, idx:idx + 1, :], g3.shape).reshape(rows, lanes)
    pos = lax.broadcasted_iota(jnp.int32, g.shape, 0) & (group - 1)
    r = g
    for m in range(group):
        off = idx - m
        if off != 0:
            r = jnp.where(pos == m, pltpu.roll(g, (-off) % rows, 0), r)
    return r


def _gla_chunks(chains):
    c = chains[0]["q"].shape[0]
    n_levels = c.bit_length()
    for ch in chains:
        hi = ch["lf2"].astype(BF16)
        lo = (ch["lf2"] - hi.astype(F32)).astype(BF16)
        ch["s"] = jnp.dot(ch["sums"], jnp.concatenate([hi, lo], axis=0), preferred_element_type=F32)
    for ch in chains:
        qd = ch["q"] * jnp.exp2(ch["s"][:c]).astype(BF16)
        ch["o"] = lax.dot_general(qd, ch["st"].astype(BF16), _NT, preferred_element_type=F32)
        ch["a"] = jnp.zeros((c, 2 * c), F32)
    for level in range(0, n_levels, 2):
        for ch in chains:
            qs, ks = [], []
            for l in (level, level + 1):
                if l == 0:
                    qs.append(ch["q"])
                    ks.append(ch["k"])
                elif l < n_levels:
                    e = jnp.exp2(-jnp.abs(ch["s"][l * c:(l + 1) * c])).astype(BF16)
                    qs.append(ch["q"] * e)
                    ks.append(ch["k"] * e)
            if len(ks) == 1:
                ks = ks * 2
            p = lax.dot_general(jnp.concatenate(qs, axis=0), jnp.concatenate(ks, axis=0), _NT,
                                preferred_element_type=F32)
            for i in range(len(qs)):
                ch["a"] = jnp.where(ch["lvl"] == 2 * (level + i) + i, p[i * c:(i + 1) * c, :], ch["a"])
    outs = []
    for ch in chains:
        g = ch["s"][:c]
        g_tot = g[c - 1:c, :] if ch["fwd"] else g[0:1, :]
        vv = jnp.concatenate([ch["v"], ch["v"]], axis=0)
        o = ch["o"] + jnp.dot(ch["a"].astype(BF16), vv, preferred_element_type=F32)
        kd = ch["k"] * jnp.exp2(g_tot - g).astype(BF16)
        st_new = (ch["st"] * jnp.exp2(g_tot)
                  + lax.dot_general(ch["v"], kd, _TN, preferred_element_type=F32))
        outs.append((o, st_new))
    return outs


def _gla_tables(c, fwd):
    ti = lax.broadcasted_iota(jnp.int32, (c, c), 0)
    ui = lax.broadcasted_iota(jnp.int32, (c, c), 1)

    def cum(row):
        return ((ui <= row) if fwd else (ui >= row)).astype(jnp.int32)

    blocks = [cum(ti)]
    half = 1
    while half < c:
        boundary = (ti & -(2 * half)) + (half - 1 if fwd else half)
        blocks.append(cum(ti) - cum(boundary))
        half *= 2
    x = ti ^ ui
    top_bit = sum((x >= (1 << b)).astype(jnp.int32) for b in range(1, c.bit_length() - 1))
    lvl = jnp.where(ti == ui, 0, jnp.where((ui < ti) if fwd else (ui > ti), 1 + top_bit, -1))
    sums = jnp.concatenate(blocks, axis=0).astype(F32).astype(BF16)
    codes = jnp.concatenate([2 * lvl, 2 * lvl + 1], axis=1)
    return jnp.concatenate([sums, sums], axis=1), codes


def _gla_body(qf_ref, kf_ref, vf_ref, lf_ref, qb_ref, kb_ref, vb_ref, lb_ref, of_ref, ob_ref,
              st_ref, sums_ref, lvl_ref, *, chunk):
    rows, width = qf_ref.shape
    nch = rows // chunk
    c = chunk

    @pl.when(pl.program_id(2) == 0)
    def _():
        st_ref[...] = jnp.zeros_like(st_ref)

    dirs = []
    for d, (fwd, refs) in enumerate(((True, (qf_ref, kf_ref, vf_ref, lf_ref, of_ref)),
                                     (False, (qb_ref, kb_ref, vb_ref, lb_ref, ob_ref)))):
        sums, lvl = _gla_tables(c, fwd)
        sums_ref[d] = sums
        lvl_ref[d] = lvl
        dirs.append((fwd, refs))

    def one_chunk(ci, carry):
        chains, dests = [], []
        for d, (fwd, (q_ref, k_ref, v_ref, l_ref, o_ref)) in enumerate(dirs):
            r0 = pl.multiple_of((ci if fwd else nch - 1 - ci) * c, c)
            for h in range(width // HEAD_DIM):
                hs = slice(h * HEAD_DIM, (h + 1) * HEAD_DIM)
                chains.append(dict(q=q_ref[pl.ds(r0, c), hs], k=k_ref[pl.ds(r0, c), hs],
                                   v=v_ref[pl.ds(r0, c), hs], lf2=l_ref[pl.ds(r0, c), hs],
                                   st=st_ref[d, h], lvl=lvl_ref[d], fwd=fwd, sums=sums_ref[d]))
                dests.append((o_ref, r0, hs, d, h))
        for (o, st_new), (o_ref, r0, hs, d, h) in zip(_gla_chunks(chains), dests):
            o_ref[pl.ds(r0, c), hs] = o.astype(o_ref.dtype)
            st_ref[d, h] = st_new
        return carry

    lax.fori_loop(0, nch, one_chunk, 0)


def gla_bidirectional(qvg, kk, log2f, *, batch, seq, heads, heads_per_step=2, rows_per_step=512):
    t = batch * seq
    hg = heads_per_step
    width = hg * HEAD_DIM
    rb = rows_per_step
    ns = seq // rb
    ng = heads // hg

    def spec(fwd, seg):
        return pl.BlockSpec((rb, width),
                            lambda b, g, s: (b * ns + (s if fwd else ns - 1 - s), seg * ng + g))

    return pl.pallas_call(
        functools.partial(_gla_body, chunk=GLA_CHUNK),
        grid=(batch, ng, ns),
        in_specs=[spec(True, 0), spec(True, 0), spec(True, 1), spec(True, 0),
                  spec(False, 0), spec(False, 1), spec(False, 1), spec(False, 1)],
        out_specs=[spec(True, 0), spec(False, 0)],
        out_shape=[jax.ShapeDtypeStruct((t, heads * HEAD_DIM), BF16)] * 2,
        scratch_shapes=[pltpu.VMEM((2, hg, HEAD_DIM, HEAD_DIM), F32),
                        pltpu.VMEM((2, GLA_CHUNK * GLA_CHUNK.bit_length(), 2 * GLA_CHUNK), BF16),
                        pltpu.VMEM((2, GLA_CHUNK, 2 * GLA_CHUNK), jnp.int32)],
        compiler_params=_params(("parallel", "parallel", "arbitrary"), 32 << 20),
        name="gla_bidir",
    )(qvg, kk, qvg, log2f, qvg, kk, qvg, log2f)


def _proj_residual_body(*refs, n_pro, prologue):
    pro, (w_ref, x_ref, o_ref, y_ref) = refs[:n_pro], refs[n_pro:]

    @pl.when(pl.program_id(1) == 0)
    def _():
        prologue(pro, y_ref)

    o_ref[...] = x_ref[...] + jnp.dot(y_ref[...], w_ref[...], preferred_element_type=F32)


def proj_residual(name, pro_inputs, pro_specs, prologue, w, xres, tm=512, tn=512):
    m, n = xres.shape
    kdim = w.shape[0]
    pro_bytes = sum(2 * tm * kdim * jnp.dtype(a.dtype).itemsize for a in pro_inputs)
    vmem = pro_bytes + tm * kdim * 2 + 2 * kdim * tn * 2 + 4 * tm * tn * 4 + 4 * tm * kdim * 4 + (4 << 20)
    return pl.pallas_call(
        functools.partial(_proj_residual_body, n_pro=len(pro_inputs), prologue=prologue),
        grid=(m // tm, n // tn),
        in_specs=[*pro_specs,
                  pl.BlockSpec((kdim, tn), lambda i, j: (0, j)),
                  pl.BlockSpec((tm, tn), lambda i, j: (i, j))],
        out_specs=pl.BlockSpec((tm, tn), lambda i, j: (i, j)),
        out_shape=jax.ShapeDtypeStruct((m, n), F32),
        scratch_shapes=[pltpu.VMEM((tm, kdim), BF16)],
        compiler_params=_params(("parallel", "arbitrary"), vmem),
        name=name,
    )(*pro_inputs, w, xres)


def _hgrn_out_prologue(pro, y_ref):
    of_ref, ob_ref, gate_ref, gain_ref = pro
    for h in range(of_ref.shape[1] // HEAD_DIM):
        hs = slice(h * HEAD_DIM, (h + 1) * HEAD_DIM)
        o = of_ref[:, hs].astype(F32) + ob_ref[:, hs].astype(F32)
        y = _rms_rows(o, gain_ref[:, hs]) * gate_ref[:, hs].astype(F32)
        y_ref[:, hs] = y.astype(BF16)


def _attn_merge_prologue(pro, y_ref):
    o_refs, l_refs = pro[:3], pro[3:]
    for h in range(y_ref.shape[1] // HEAD_DIM):
        hs = slice(h * HEAD_DIM, (h + 1) * HEAD_DIM)
        ls = [r[:, hs] for r in l_refs]
        m = jnp.maximum(jnp.maximum(ls[0], ls[1]), ls[2])
        es = [jnp.exp(l - m) for l in ls]
        den = es[0] + es[1] + es[2]
        o = sum((e / den) * r[:, hs].astype(F32) for e, r in zip(es, o_refs))
        y_ref[:, hs] = o.astype(BF16)


def _swiglu_body(x_ref, g_ref, wg_ref, wu_ref, wd_ref, o_ref, hn_ref):
    @pl.when(pl.program_id(1) == 0)
    def _():
        x = x_ref[...]
        hn_ref[...] = _rms_rows(x, g_ref[...]).astype(BF16)
        o_ref[...] = x

    hn = hn_ref[...]
    a = jnp.dot(hn, wg_ref[...], preferred_element_type=F32)
    u = jnp.dot(hn, wu_ref[...], preferred_element_type=F32)
    h = (_silu(a) * u).astype(BF16)
    o_ref[...] += jnp.dot(h, wd_ref[...], preferred_element_type=F32)


def norm_swiglu_residual(x, gain, wg, wu, wd, tm=512, tf=512):
    m, d = x.shape
    f = wg.shape[1]
    assert m % tm == 0 and f % tf == 0
    vmem = 4 * tm * d * 4 + tm * d * 2 + 2 * (2 * d * tf + tf * d) * 2 + 3 * tm * tf * 4 + 2 * tm * d * 4 + (4 << 20)
    return pl.pallas_call(
        _swiglu_body,
        grid=(m // tm, f // tf),
        in_specs=[pl.BlockSpec((tm, d), lambda i, j: (i, 0)),
                  pl.BlockSpec((1, d), lambda i, j: (0, 0)),
                  pl.BlockSpec((d, tf), lambda i, j: (0, j)),
                  pl.BlockSpec((d, tf), lambda i, j: (0, j)),
                  pl.BlockSpec((tf, d), lambda i, j: (j, 0))],
        out_specs=pl.BlockSpec((tm, d), lambda i, j: (i, 0)),
        out_shape=jax.ShapeDtypeStruct((m, d), F32),
        scratch_shapes=[pltpu.VMEM((tm, d), BF16)],
        compiler_params=_params(("parallel", "arbitrary"), vmem),
        name="dense_swiglu",
    )(x, gain.reshape(1, d), wg, wu, wd)


def _band_attn_body(q_ref, k_ref, v_ref, o_ref, l_ref, *, steps, tq):
    n, width = q_ref.shape
    wk = min(n, tq + 2 * steps)
    nqb = n // tq

    def one_block(qi, carry):
        q0 = pl.multiple_of(qi * tq, tq)
        k0 = pl.multiple_of(jnp.clip(q0 - steps, 0, n - wk), steps)
        qpos = q0 + lax.broadcasted_iota(jnp.int32, (tq, wk), 0)
        kpos = k0 + lax.broadcasted_iota(jnp.int32, (tq, wk), 1)
        valid = jnp.abs(kpos - qpos) <= steps
        for h in range(width // HEAD_DIM):
            hs = slice(h * HEAD_DIM, (h + 1) * HEAD_DIM)
            q = q_ref[pl.ds(q0, tq), hs]
            kw = k_ref[pl.ds(k0, wk), hs]
            vw = v_ref[pl.ds(k0, wk), hs]
            s = lax.dot_general(q, kw, _NT, preferred_element_type=F32)
            s = jnp.where(valid, s, NEG_INF)
            m = jnp.max(s, axis=-1, keepdims=True)
            p = jnp.exp(s - m)
            den = jnp.sum(p, axis=-1, keepdims=True)
            o = jnp.dot(p.astype(BF16), vw, preferred_element_type=F32) / den
            o_ref[pl.ds(q0, tq), hs] = o.astype(o_ref.dtype)
            l_ref[pl.ds(q0, tq), hs] = jnp.broadcast_to(m + jnp.log(den), (tq, HEAD_DIM))
        return carry

    lax.fori_loop(0, nqb, one_block, 0)


def band_attention(qkv, *, batch, seq, heads, dilation, steps, heads_per_step, tq=256):
    t = batch * seq
    r = dilation
    n = seq // r
    d = heads * HEAD_DIM
    hg = heads_per_step
    width = hg * HEAD_DIM
    view = qkv.reshape(t // r, r * 3 * d)
    per_class = 3 * d // width
    per_part = d // width
    tq = min(tq, n)

    def col(part):
        return lambda b, c, g: (b, c * per_class + part * per_part + g)

    out_col = lambda b, c, g: (b, c * per_part + g)
    o, lse = pl.pallas_call(
        functools.partial(_band_attn_body, steps=steps, tq=tq),
        grid=(batch, r, heads // hg),
        in_specs=[pl.BlockSpec((n, width), col(0)),
                  pl.BlockSpec((n, width), col(1)),
                  pl.BlockSpec((n, width), col(2))],
        out_specs=[pl.BlockSpec((n, width), out_col), pl.BlockSpec((n, width), out_col)],
        out_shape=[jax.ShapeDtypeStruct((t // r, r * d), BF16),
                   jax.ShapeDtypeStruct((t // r, r * d), F32)],
        compiler_params=_params(("parallel", "parallel", "parallel"),
                                2 * n * width * (3 * 2 + 2 + 4) + (8 << 20)),
        name=f"band_attn_r{r}",
    )(view, view, view)
    return o.reshape(t, d), lse.reshape(t, d)


def _router_body(x_ref, g_ref, wr_ref, hn_ref, e_ref, p_ref):
    hn = _rms_rows(x_ref[...], g_ref[...])
    hn_ref[...] = hn
    logits = jnp.dot(hn, wr_ref[...], precision=lax.Precision.HIGHEST, preferred_element_type=F32)
    lane = lax.broadcasted_iota(jnp.int32, logits.shape, 1)
    logits = jnp.where(lane < N_EXPERTS, logits, -jnp.inf)
    m1 = jnp.max(logits, axis=-1, keepdims=True)
    i1 = jnp.min(jnp.where(logits == m1, lane, LANES), axis=-1, keepdims=True)
    rest = jnp.where(lane == i1, -jnp.inf, logits)
    m2 = jnp.max(rest, axis=-1, keepdims=True)
    i2 = jnp.min(jnp.where(rest == m2, lane, LANES), axis=-1, keepdims=True)
    e2 = jnp.exp(m2 - m1)
    den = 1.0 + e2
    e_ref[...] = jnp.where(lane == 0, i1, jnp.where(lane == 1, i2, 0))
    p_ref[...] = jnp.where(lane == 0, 1.0 / den, jnp.where(lane == 1, e2 / den, 0.0))


def moe_router(x, gain, w_router_padded, tm=512):
    m, d = x.shape
    return pl.pallas_call(
        _router_body,
        grid=(m // tm,),
        in_specs=[pl.BlockSpec((tm, d), lambda i: (i, 0)),
                  pl.BlockSpec((1, d), lambda i: (0, 0)),
                  pl.BlockSpec((d, LANES), lambda i: (0, 0))],
        out_specs=[pl.BlockSpec((tm, d), lambda i: (i, 0)),
                   pl.BlockSpec((tm, LANES), lambda i: (i, 0)),
                   pl.BlockSpec((tm, LANES), lambda i: (i, 0))],
        out_shape=[jax.ShapeDtypeStruct((m, d), F32),
                   jax.ShapeDtypeStruct((m, LANES), jnp.int32),
                   jax.ShapeDtypeStruct((m, LANES), F32)],
        compiler_params=_params(("parallel",), 6 * tm * d * 4 + (8 << 20)),
        name="moe_router",
    )(x, gain.reshape(1, d), w_router_padded)


def _expert_body(blk_e_ref, n_used_ref, tok_ref, gate_ref, hn_hbm, wg_ref, wu_ref, wd_ref,
                 y_ref, xf_ref, xb_ref, sem):
    b = pl.program_id(0)
    f = pl.program_id(1)
    nf = pl.num_programs(1)
    tb = y_ref.shape[0]
    used = b < n_used_ref[0]

    def row_copy(i):
        return pltpu.make_async_copy(hn_hbm.at[pl.ds(tok_ref[0, i], 1), :],
                                     xf_ref.at[pl.ds(i, 1), :], sem)

    @pl.when(used & (f == 0))
    def _():
        def start(i, c):
            row_copy(i).start()
            return c

        def wait(i, c):
            row_copy(i).wait()
            return c

        lax.fori_loop(0, tb, start, 0)
        lax.fori_loop(0, tb, wait, 0)
        xb_ref[...] = xf_ref[...].astype(BF16)
        y_ref[...] = jnp.zeros_like(y_ref)

    @pl.when(~used & (f == 0))
    def _():
        y_ref[...] = jnp.zeros_like(y_ref)

    @pl.when(used)
    def _():
        xb = xb_ref[...]
        a = jnp.dot(xb, wg_ref[...], preferred_element_type=F32)
        u = jnp.dot(xb, wu_ref[...], preferred_element_type=F32)
        h = (_silu(a) * u).astype(BF16)
        y_ref[...] += jnp.dot(h, wd_ref[...], preferred_element_type=F32)

    @pl.when(used & (f == nf - 1))
    def _():
        y_ref[...] = y_ref[...] * gate_ref[...]


def moe_experts(hn, slot_tok, slot_gate, blk_e, n_used, wg, wu, wd, tb, tf=512):
    t, d = hn.shape
    p = slot_tok.shape[0]
    nblk = p // tb
    fdim = wg.shape[2]
    nf = fdim // tf

    def live(b, n_used_ref):
        return jnp.minimum(b, n_used_ref[0] - 1)

    def wmap_cols(b, f, blk_e_ref, n_used_ref):
        return (blk_e_ref[live(b, n_used_ref)], 0, jnp.where(b < n_used_ref[0], f, nf - 1))

    def wmap_rows(b, f, blk_e_ref, n_used_ref):
        return (blk_e_ref[live(b, n_used_ref)], jnp.where(b < n_used_ref[0], f, nf - 1), 0)

    vmem = (tb * d * (4 + 2) + 2 * tb * d * 4 + 2 * (2 * d * tf + tf * d) * 2 + 3 * tb * tf * 4
            + 2 * tb * d * 4 + (4 << 20))
    grid_spec = pltpu.PrefetchScalarGridSpec(
        num_scalar_prefetch=2,
        grid=(nblk, nf),
        in_specs=[pl.BlockSpec((None, 1, tb), lambda b, f, *_: (b, 0, 0), memory_space=pltpu.SMEM),
                  pl.BlockSpec((tb, 1), lambda b, f, *_: (b, 0)),
                  pl.BlockSpec(memory_space=pl.ANY),
                  pl.BlockSpec((None, d, tf), wmap_cols),
                  pl.BlockSpec((None, d, tf), wmap_cols),
                  pl.BlockSpec((None, tf, d), wmap_rows)],
        out_specs=pl.BlockSpec((tb, d), lambda b, f, *_: (b, 0)),
        scratch_shapes=[pltpu.VMEM((tb, d), F32), pltpu.VMEM((tb, d), BF16), pltpu.SemaphoreType.DMA(())],
    )
    return pl.pallas_call(
        _expert_body,
        grid_spec=grid_spec,
        out_shape=jax.ShapeDtypeStruct((p, d), F32),
        compiler_params=_params(("arbitrary", "arbitrary"), vmem),
        name="moe_experts",
    )(blk_e, n_used, slot_tok.reshape(nblk, 1, tb), slot_gate.reshape(p, 1), hn, wg, wu, wd)


def _combine_body(slot_ref, x_ref, y_hbm, o_ref, buf_ref, sem):
    tm = x_ref.shape[0]

    def row_copy(i):
        k = i // tm
        r = i - k * tm
        return pltpu.make_async_copy(y_hbm.at[pl.ds(slot_ref[0, i], 1), :],
                                     buf_ref.at[k, pl.ds(r, 1), :], sem)

    def start(i, c):
        row_copy(i).start()
        return c

    def wait(i, c):
        row_copy(i).wait()
        return c

    lax.fori_loop(0, TOP_K * tm, start, 0)
    lax.fori_loop(0, TOP_K * tm, wait, 0)
    o_ref[...] = x_ref[...] + (buf_ref[0] + buf_ref[1])


def moe_combine(x, y, tok_slots, tm=256):
    t, d = x.shape
    nt = t // tm
    slots = tok_slots.reshape(nt, tm, TOP_K).transpose(0, 2, 1).reshape(nt, 1, TOP_K * tm)
    return pl.pallas_call(
        _combine_body,
        grid=(nt,),
        in_specs=[pl.BlockSpec((None, 1, TOP_K * tm), lambda i: (i, 0, 0), memory_space=pltpu.SMEM),
                  pl.BlockSpec((tm, d), lambda i: (i, 0)),
                  pl.BlockSpec(memory_space=pl.ANY)],
        out_specs=pl.BlockSpec((tm, d), lambda i: (i, 0)),
        out_shape=jax.ShapeDtypeStruct((t, d), F32),
        scratch_shapes=[pltpu.VMEM((TOP_K, tm, d), F32), pltpu.SemaphoreType.DMA(())],
        compiler_params=_params(("arbitrary",), (4 + TOP_K) * tm * d * 4 + (4 << 20)),
        name="moe_combine",
    )(slots, x, y)


def _routing_plan(top_e, gates, tb):
    t = top_e.shape[0]
    n = t * TOP_K
    e_flat = top_e.reshape(n)
    onehot = (e_flat[:, None] == jnp.arange(N_EXPERTS, dtype=jnp.int32)[None, :]).astype(jnp.int32)
    rank = jnp.take_along_axis(jnp.cumsum(onehot, axis=0) - onehot, e_flat[:, None], axis=1)[:, 0]
    counts = jnp.sum(onehot, axis=0)
    padded = (counts + tb - 1) // tb * tb
    pend = jnp.cumsum(padded)
    dest = (pend - padded)[e_flat] + rank
    nblk = -(-n // tb) + N_EXPERTS
    p = nblk * tb
    tok_flat = jnp.arange(n, dtype=jnp.int32) // TOP_K
    slot_tok = jnp.zeros((p,), jnp.int32).at[dest].set(tok_flat)
    slot_gate = jnp.zeros((p,), F32).at[dest].set(gates.reshape(n))
    blk_start = jnp.arange(nblk, dtype=pend.dtype) * tb
    blk_e = jnp.minimum(jnp.searchsorted(pend, blk_start, side='right'), N_EXPERTS - 1).astype(jnp.int32)
    n_used = (pend[-1] // tb).astype(jnp.int32).reshape(1)
    return slot_tok, slot_gate, blk_e, n_used, dest.reshape(t, TOP_K).astype(jnp.int32)


def _rope_tables(seq, hd):
    half = hd // 2
    inv_freq = ROPE_THETA ** (-jnp.arange(half, dtype=F32) * 2.0 / hd)
    ang = jnp.arange(seq, dtype=F32)[:, None] * inv_freq[None, :]
    cos, sin = jnp.cos(ang), jnp.sin(ang)
    return jnp.concatenate([cos, cos], axis=-1), jnp.concatenate([-sin, sin], axis=-1)


def kernel(x, norm_gains, hgrn_w_in, hgrn_lb_logits, hgrn_onorm, hgrn_w_out, attn_w_qkv, attn_q_gain,
           attn_k_gain, attn_w_out, ffn_w_gate, ffn_w_up, ffn_w_down, moe_w_router, moe_w_gate, moe_w_up,
           moe_w_down):
    batch, seq, d = x.shape
    t = batch * seq
    heads = d // HEAD_DIM
    hd = HEAD_DIM
    xf = x.reshape(t, d)

    w_in = hgrn_w_in[0]
    w_qvg = jnp.concatenate([w_in[:, :d], w_in[:, 3 * d:5 * d]], axis=1).astype(BF16)
    w_f = w_in[:, d:3 * d].astype(BF16)
    lb = jnp.cumsum(jax.nn.softmax(hgrn_lb_logits.astype(F32), axis=0), axis=0)[0].reshape(1, 2 * d)
    tn = 512
    nq = d // tn

    def qvg_epilogue(acc, j, aux, outs):
        is_v = (j >= nq) & (j < 2 * nq)
        outs[0][...] = jnp.where(is_v, acc, _silu(acc)).astype(BF16)

    (qvg,) = norm_matmul(
        "hgrn_qvg_proj", xf, norm_gains[0, 0], w_qvg, qvg_epilogue,
        [jax.ShapeDtypeStruct((t, 3 * d), BF16)], [pl.BlockSpec((1024, tn), lambda i, j: (i, j))], tn=tn)

    def f_epilogue(acc, j, aux, outs):
        lbv = aux[0][...]
        fgate = lbv + (1.0 - lbv) * jax.nn.sigmoid(acc)
        outs[0][...] = jnp.log2(fgate)
        outs[1][...] = (1.0 - fgate).astype(BF16)

    log2f, kk = norm_matmul(
        "hgrn_forget_proj", xf, norm_gains[0, 0], w_f, f_epilogue,
        [jax.ShapeDtypeStruct((t, 2 * d), F32), jax.ShapeDtypeStruct((t, 2 * d), BF16)],
        [pl.BlockSpec((1024, tn), lambda i, j: (i, j)), pl.BlockSpec((1024, tn), lambda i, j: (i, j))],
        aux=(lb,), aux_specs=(pl.BlockSpec((1, tn), lambda i, j: (0, j)),), tn=tn)

    o_f, o_b = gla_bidirectional(qvg, kk, log2f, batch=batch, seq=seq, heads=heads)

    tm = 512
    x1 = proj_residual(
        "hgrn_out_proj", (o_f, o_b, qvg, hgrn_onorm[0].reshape(1, d)),
        (pl.BlockSpec((tm, d), lambda i, j: (i, 0)), pl.BlockSpec((tm, d), lambda i, j: (i, 0)),
         pl.BlockSpec((tm, d), lambda i, j: (i, 2)), pl.BlockSpec((1, d), lambda i, j: (0, 0))),
        _hgrn_out_prologue, hgrn_w_out[0].astype(BF16), xf, tm=tm)

    fdim = ffn_w_gate.shape[2]
    fpad = -(-fdim // 512) * 512 - fdim
    wg = jnp.pad(ffn_w_gate[0], ((0, 0), (0, fpad))).astype(BF16)
    wu = jnp.pad(ffn_w_up[0], ((0, 0), (0, fpad))).astype(BF16)
    wd = jnp.pad(ffn_w_down[0], ((0, fpad), (0, 0))).astype(BF16)
    x2 = norm_swiglu_residual(x1, norm_gains[0, 1], wg, wu, wd)

    cos, sin = _rope_tables(seq, hd)
    qg = attn_q_gain[0].reshape(1, hd)
    kg = attn_k_gain[0].reshape(1, hd)
    tm_qkv = 1024
    pos_blocks = seq // tm_qkv

    def qkv_epilogue(acc, j, aux, outs):
        cos_ref, sin_ref, qg_ref, kg_ref = aux
        o_ref = outs[0]

        def normed_rope(gain, scale):
            for h in range(tn // hd):
                hs = slice(h * hd, (h + 1) * hd)
                y = _rms_rows(acc[:, hs], gain)
                y = y * cos_ref[...] + pltpu.roll(y, hd // 2, 1) * sin_ref[...]
                o_ref[:, hs] = (y * scale).astype(BF16)

        @pl.when(j < nq)
        def _():
            normed_rope(qg_ref[...], hd ** -0.5)

        @pl.when((j >= nq) & (j < 2 * nq))
        def _():
            normed_rope(kg_ref[...], 1.0)

        @pl.when(j >= 2 * nq)
        def _():
            o_ref[...] = acc.astype(BF16)

    (qkv,) = norm_matmul(
        "attn_qkv_proj", x2, norm_gains[1, 0], attn_w_qkv[0].astype(BF16), qkv_epilogue,
        [jax.ShapeDtypeStruct((t, 3 * d), BF16)], [pl.BlockSpec((tm_qkv, tn), lambda i, j: (i, j))],
        aux=(cos, sin, qg, kg),
        aux_specs=(pl.BlockSpec((tm_qkv, hd), lambda i, j: (i % pos_blocks, 0)),
                   pl.BlockSpec((tm_qkv, hd), lambda i, j: (i % pos_blocks, 0)),
                   pl.BlockSpec((1, hd), lambda i, j: (0, 0)),
                   pl.BlockSpec((1, hd), lambda i, j: (0, 0))),
        tm=tm_qkv, tn=tn)

    branch_o, branch_l = [], []
    for (window, dil), hg in zip(DIL_BRANCHES, (1, 2, 4)):
        o_br, l_br = band_attention(qkv, batch=batch, seq=seq, heads=heads, dilation=dil,
                                    steps=window // (2 * dil), heads_per_step=hg)
        branch_o.append(o_br)
        branch_l.append(l_br)

    tm_merge = 256
    row_spec = pl.BlockSpec((tm_merge, d), lambda i, j: (i, 0))
    x3 = proj_residual("attn_out_proj", (*branch_o, *branch_l), (row_spec,) * 6, _attn_merge_prologue,
                       attn_w_out[0].astype(BF16), x2, tm=tm_merge)

    tb = 512
    w_router = jnp.pad(moe_w_router[0].astype(F32), ((0, 0), (0, LANES - N_EXPERTS)))
    hn3, e_pad, p_pad = moe_router(x3, norm_gains[1, 1], w_router)
    slot_tok, slot_gate, blk_e, n_used, tok_slots = _routing_plan(e_pad[:, :TOP_K], p_pad[:, :TOP_K], tb)
    y = moe_experts(hn3, slot_tok, slot_gate, blk_e, n_used, moe_w_gate[0].astype(BF16),
                    moe_w_up[0].astype(BF16), moe_w_down[0].astype(BF16), tb)
    out = moe_combine(x3, y, tok_slots)
    return out.reshape(batch, seq, d)
```

```python
import functools

import jax
import jax.numpy as jnp
from jax import lax
from jax.experimental import pallas as pl
from jax.experimental.pallas import tpu as pltpu

F32 = jnp.float32
BF16 = jnp.bfloat16
EPS = 1e-6
NEG_INF = -1e30
ROPE_THETA = 10000.0

HEAD_DIM = 128
GLA_CHUNK = 64
DIL_BRANCHES = ((128, 1), (512, 4), (2048, 16))
N_EXPERTS = 8
TOP_K = 2

LANES = 128
V7X_VMEM_BYTES = 64 * 1024 * 1024
VMEM_BUDGET = 56 * 1024 * 1024

_NT = (((1,), (1,)), ((), ()))
_TN = (((0,), (0,)), ((), ()))


def _params(semantics, vmem_bytes):
    return pltpu.CompilerParams(dimension_semantics=semantics,
                                vmem_limit_bytes=int(min(vmem_bytes, VMEM_BUDGET)))


def _silu(x):
    return x * jax.nn.sigmoid(x)


def _rms_rows(x, gain):
    ms = jnp.mean(x * x, axis=-1, keepdims=True)
    return x * lax.rsqrt(ms + EPS) * gain


def _norm_matmul_body(x_ref, g_ref, w_ref, *rest, n_aux, epilogue):
    aux, outs, hn_ref = rest[:n_aux], rest[n_aux:-1], rest[-1]
    j = pl.program_id(1)

    @pl.when(j == 0)
    def _():
        hn_ref[...] = _rms_rows(x_ref[...], g_ref[...]).astype(BF16)

    acc = jnp.dot(hn_ref[...], w_ref[...], preferred_element_type=F32)
    epilogue(acc, j, aux, outs)


def norm_matmul(name, x, gain, w, epilogue, out_shapes, out_specs, aux=(), aux_specs=(), tm=1024, tn=512):
    m, d = x.shape
    n = w.shape[1]
    assert m % tm == 0 and n % tn == 0
    out_bytes = sum(2 * tm * tn * jnp.dtype(s.dtype).itemsize for s in out_shapes)
    vmem = 2 * tm * d * 4 + tm * d * 2 + 2 * d * tn * 2 + out_bytes + 4 * tm * tn * 4 + (4 << 20)
    return pl.pallas_call(
        functools.partial(_norm_matmul_body, n_aux=len(aux), epilogue=epilogue),
        grid=(m // tm, n // tn),
        in_specs=[pl.BlockSpec((tm, d), lambda i, j: (i, 0)),
                  pl.BlockSpec((1, d), lambda i, j: (0, 0)),
                  pl.BlockSpec((d, tn), lambda i, j: (0, j)),
                  *aux_specs],
        out_specs=out_specs,
        out_shape=out_shapes,
        scratch_shapes=[pltpu.VMEM((tm, d), BF16)],
        compiler_params=_params(("parallel", "arbitrary"), vmem),
        name=name,
    )(x, gain.reshape(1, d), w, *aux)


def _group_row_bcast(g, group, idx):
    rows, lanes = g.shape
    if group >= 8:
        g3 = g.reshape(rows // group, group, lanes)
        return jnp.broadcast_to(g3[:, idx:idx + 1, :], g3.shape).reshape(rows, lanes)
    pos = lax.broadcasted_iota(jnp.int32, g.shape, 0) & (group - 1)
    r = g
    for m in range(group):
        off = idx - m
        if off != 0:
            r = jnp.where(pos == m, pltpu.roll(g, (-off) % rows, 0), r)
    return r


def _gla_chunks(chains):
    c = chains[0]["q"].shape[0]
    n_levels = c.bit_length()
    for ch in chains:
        hi = ch["lf2"].astype(BF16)
        lo = (ch["lf2"] - hi.astype(F32)).astype(BF16)
        ch["s"] = jnp.dot(ch["sums"], jnp.concatenate([hi, lo], axis=0), preferred_element_type=F32)
    for ch in chains:
        qd = ch["q"] * jnp.exp2(ch["s"][:c]).astype(BF16)
        ch["o"] = lax.dot_general(qd, ch["st"].astype(BF16), _NT, preferred_element_type=F32)
        ch["a"] = jnp.zeros((c, 2 * c), F32)
    for level in range(0, n_levels, 2):
        for ch in chains:
            qs, ks = [], []
            for l in (level, level + 1):
                if l == 0:
                    qs.append(ch["q"])
                    ks.append(ch["k"])
                elif l < n_levels:
                    e = jnp.exp2(-jnp.abs(ch["s"][l * c:(l + 1) * c])).astype(BF16)
                    qs.append(ch["q"] * e)
                    ks.append(ch["k"] * e)
            if len(ks) == 1:
                ks = ks * 2
            p = lax.dot_general(jnp.concatenate(qs, axis=0), jnp.concatenate(ks, axis=0), _NT,
                                preferred_element_type=F32)
            for i in range(len(qs)):
                ch["a"] = jnp.where(ch["lvl"] == 2 * (level + i) + i, p[i * c:(i + 1) * c, :], ch["a"])
    outs = []
    for ch in chains:
        g = ch["s"][:c]
        g_tot = g[c - 1:c, :] if ch["fwd"] else g[0:1, :]
        vv = jnp.concatenate([ch["v"], ch["v"]], axis=0)
        o = ch["o"] + jnp.dot(ch["a"].astype(BF16), vv, preferred_element_type=F32)
        kd = ch["k"] * jnp.exp2(g_tot - g).astype(BF16)
        st_new = (ch["st"] * jnp.exp2(g_tot)
                  + lax.dot_general(ch["v"], kd, _TN, preferred_element_type=F32))
        outs.append((o, st_new))
    return outs


def _gla_tables(c, fwd):
    ti = lax.broadcasted_iota(jnp.int32, (c, c), 0)
    ui = lax.broadcasted_iota(jnp.int32, (c, c), 1)

    def cum(row):
        return ((ui <= row) if fwd else (ui >= row)).astype(jnp.int32)

    blocks = [cum(ti)]
    half = 1
    while half < c:
        boundary = (ti & -(2 * half)) + (half - 1 if fwd else half)
        blocks.append(cum(ti) - cum(boundary))
        half *= 2
    x = ti ^ ui
    top_bit = sum((x >= (1 << b)).astype(jnp.int32) for b in range(1, c.bit_length() - 1))
    lvl = jnp.where(ti == ui, 0, jnp.where((ui < ti) if fwd else (ui > ti), 1 + top_bit, -1))
    sums = jnp.concatenate(blocks, axis=0).astype(F32).astype(BF16)
    codes = jnp.concatenate([2 * lvl, 2 * lvl + 1], axis=1)
    return jnp.concatenate([sums, sums], axis=1), codes


def _gla_body(qf_ref, kf_ref, vf_ref, lf_ref, qb_ref, kb_ref, vb_ref, lb_ref, of_ref, ob_ref,
              st_ref, sums_ref, lvl_ref, *, chunk):
    rows, width = qf_ref.shape
    nch = rows // chunk
    c = chunk

    @pl.when(pl.program_id(2) == 0)
    def _():
        st_ref[...] = jnp.zeros_like(st_ref)

    dirs = []
    for d, (fwd, refs) in enumerate(((True, (qf_ref, kf_ref, vf_ref, lf_ref, of_ref)),
                                     (False, (qb_ref, kb_ref, vb_ref, lb_ref, ob_ref)))):
        sums, lvl = _gla_tables(c, fwd)
        sums_ref[d] = sums
        lvl_ref[d] = lvl
        dirs.append((fwd, refs))

    def one_chunk(ci, carry):
        chains, dests = [], []
        for d, (fwd, (q_ref, k_ref, v_ref, l_ref, o_ref)) in enumerate(dirs):
            r0 = pl.multiple_of((ci if fwd else nch - 1 - ci) * c, c)
            for h in range(width // HEAD_DIM):
                hs = slice(h * HEAD_DIM, (h + 1) * HEAD_DIM)
                chains.append(dict(q=q_ref[pl.ds(r0, c), hs], k=k_ref[pl.ds(r0, c), hs],
                                   v=v_ref[pl.ds(r0, c), hs], lf2=l_ref[pl.ds(r0, c), hs],
                                   st=st_ref[d, h], lvl=lvl_ref[d], fwd=fwd, sums=sums_ref[d]))
                dests.append((o_ref, r0, hs, d, h))
        for (o, st_new), (o_ref, r0, hs, d, h) in zip(_gla_chunks(chains), dests):
            o_ref[pl.ds(r0, c), hs] = o.astype(o_ref.dtype)
            st_ref[d, h] = st_new
        return carry

    lax.fori_loop(0, nch, one_chunk, 0)


def gla_bidirectional(qvg, kk, log2f, *, batch, seq, heads, heads_per_step=2, rows_per_step=512):
    t = batch * seq
    hg = heads_per_step
    width = hg * HEAD_DIM
    rb = rows_per_step
    ns = seq // rb
    ng = heads // hg

    def spec(fwd, seg):
        return pl.BlockSpec((rb, width),
                            lambda b, g, s: (b * ns + (s if fwd else ns - 1 - s), seg * ng + g))

    return pl.pallas_call(
        functools.partial(_gla_body, chunk=GLA_CHUNK),
        grid=(batch, ng, ns),
        in_specs=[spec(True, 0), spec(True, 0), spec(True, 1), spec(True, 0),
                  spec(False, 0), spec(False, 1), spec(False, 1), spec(False, 1)],
        out_specs=[spec(True, 0), spec(False, 0)],
        out_shape=[jax.ShapeDtypeStruct((t, heads * HEAD_DIM), BF16)] * 2,
        scratch_shapes=[pltpu.VMEM((2, hg, HEAD_DIM, HEAD_DIM), F32),
                        pltpu.VMEM((2, GLA_CHUNK * GLA_CHUNK.bit_length(), 2 * GLA_CHUNK), BF16),
                        pltpu.VMEM((2, GLA_CHUNK, 2 * GLA_CHUNK), jnp.int32)],
        compiler_params=_params(("parallel", "parallel", "arbitrary"), 32 << 20),
        name="gla_bidir",
    )(qvg, kk, qvg, log2f, qvg, kk, qvg, log2f)


def _proj_residual_body(*refs, n_pro, prologue):
    pro, (w_ref, x_ref, o_ref, y_ref) = refs[:n_pro], refs[n_pro:]

    @pl.when(pl.program_id(1) == 0)
    def _():
        prologue(pro, y_ref)

    o_ref[...] = x_ref[...] + jnp.dot(y_ref[...], w_ref[...], preferred_element_type=F32)


def proj_residual(name, pro_inputs, pro_specs, prologue, w, xres, tm=512, tn=512):
    m, n = xres.shape
    kdim = w.shape[0]
    pro_bytes = sum(2 * tm * kdim * jnp.dtype(a.dtype).itemsize for a in pro_inputs)
    vmem = pro_bytes + tm * kdim * 2 + 2 * kdim * tn * 2 + 4 * tm * tn * 4 + 4 * tm * kdim * 4 + (4 << 20)
    return pl.pallas_call(
        functools.partial(_proj_residual_body, n_pro=len(pro_inputs), prologue=prologue),
        grid=(m // tm, n // tn),
        in_specs=[*pro_specs,
                  pl.BlockSpec((kdim, tn), lambda i, j: (0, j)),
                  pl.BlockSpec((tm, tn), lambda i, j: (i, j))],
        out_specs=pl.BlockSpec((tm, tn), lambda i, j: (i, j)),
        out_shape=jax.ShapeDtypeStruct((m, n), F32),
        scratch_shapes=[pltpu.VMEM((tm, kdim), BF16)],
        compiler_params=_params(("parallel", "arbitrary"), vmem),
        name=name,
    )(*pro_inputs, w, xres)


def _hgrn_out_prologue(pro, y_ref):
    of_ref, ob_ref, gate_ref, gain_ref = pro
    for h in range(of_ref.shape[1] // HEAD_DIM):
        hs = slice(h * HEAD_DIM, (h + 1) * HEAD_DIM)
        o = of_ref[:, hs].astype(F32) + ob_ref[:, hs].astype(F32)
        y = _rms_rows(o, gain_ref[:, hs]) * gate_ref[:, hs].astype(F32)
        y_ref[:, hs] = y.astype(BF16)


def _copy_prologue(pro, y_ref):
    y_ref[...] = pro[0][...]


def _swiglu_body(x_ref, g_ref, wg_ref, wu_ref, wd_ref, o_ref, hn_ref):
    @pl.when(pl.program_id(1) == 0)
    def _():
        x = x_ref[...]
        hn_ref[...] = _rms_rows(x, g_ref[...]).astype(BF16)
        o_ref[...] = x

    hn = hn_ref[...]
    a = jnp.dot(hn, wg_ref[...], preferred_element_type=F32)
    u = jnp.dot(hn, wu_ref[...], preferred_element_type=F32)
    h = (_silu(a) * u).astype(BF16)
    o_ref[...] += jnp.dot(h, wd_ref[...], preferred_element_type=F32)


def norm_swiglu_residual(x, gain, wg, wu, wd, tm=512, tf=512):
    m, d = x.shape
    f = wg.shape[1]
    assert m % tm == 0 and f % tf == 0
    vmem = 4 * tm * d * 4 + tm * d * 2 + 2 * (2 * d * tf + tf * d) * 2 + 3 * tm * tf * 4 + 2 * tm * d * 4 + (4 << 20)
    return pl.pallas_call(
        _swiglu_body,
        grid=(m // tm, f // tf),
        in_specs=[pl.BlockSpec((tm, d), lambda i, j: (i, 0)),
                  pl.BlockSpec((1, d), lambda i, j: (0, 0)),
                  pl.BlockSpec((d, tf), lambda i, j: (0, j)),
                  pl.BlockSpec((d, tf), lambda i, j: (0, j)),
                  pl.BlockSpec((tf, d), lambda i, j: (j, 0))],
        out_specs=pl.BlockSpec((tm, d), lambda i, j: (i, 0)),
        out_shape=jax.ShapeDtypeStruct((m, d), F32),
        scratch_shapes=[pltpu.VMEM((tm, d), BF16)],
        compiler_params=_params(("parallel", "arbitrary"), vmem),
        name="dense_swiglu",
    )(x, gain.reshape(1, d), wg, wu, wd)


def _dilated_attn_body(q_ref, k_ref, v_ref, o_ref, acc_ref, m_ref, l_ref, *, branches, tq):
    seq = q_ref.shape[0]

    for bi, (r, steps) in enumerate(branches):
        n = seq // r
        tqb = min(tq, n)
        wk = min(n, tqb + 2 * steps)
        nqb = n // tqb

        def block(idx, carry, bi=bi, r=r, steps=steps, n=n, tqb=tqb, wk=wk, nqb=nqb):
            rho = idx // nqb
            q0 = (idx - rho * nqb) * tqb
            k0 = jnp.clip(q0 - steps, 0, n - wk)

            def rows(c0, cnt):
                if r == 1:
                    return pl.ds(pl.multiple_of(c0, steps), cnt)
                return pl.ds(rho + r * c0, cnt, stride=r)

            q = q_ref[rows(q0, tqb), :].astype(BF16)
            kw = k_ref[rows(k0, wk), :].astype(BF16)
            vw = v_ref[rows(k0, wk), :].astype(BF16)
            s = lax.dot_general(q, kw, _NT, preferred_element_type=F32)
            qpos = q0 + lax.broadcasted_iota(jnp.int32, (tqb, wk), 0)
            kpos = k0 + lax.broadcasted_iota(jnp.int32, (tqb, wk), 1)
            s = jnp.where(jnp.abs(kpos - qpos) <= steps, s, NEG_INF)
            m_new = jnp.broadcast_to(jnp.max(s, axis=-1, keepdims=True), (tqb, HEAD_DIM))
            p = jnp.exp(s - m_new[:, :1])
            l_new = jnp.broadcast_to(jnp.sum(p, axis=-1, keepdims=True), (tqb, HEAD_DIM))
            acc_new = jnp.dot(p.astype(BF16), vw, preferred_element_type=F32)
            if bi > 0:
                m_old = m_ref[rows(q0, tqb), :]
                m_all = jnp.maximum(m_old, m_new)
                w_old = jnp.exp(m_old - m_all)
                w_new = jnp.exp(m_new - m_all)
                acc_new = acc_ref[rows(q0, tqb), :] * w_old + acc_new * w_new
                l_new = l_ref[rows(q0, tqb), :] * w_old + l_new * w_new
                m_new = m_all
            acc_ref[rows(q0, tqb), :] = acc_new
            l_ref[rows(q0, tqb), :] = l_new
            m_ref[rows(q0, tqb), :] = m_new
            return carry

        lax.fori_loop(0, r * nqb, block, 0)

    o_ref[...] = (acc_ref[...] / l_ref[...]).astype(o_ref.dtype)


def dilated_attention(qkv, *, batch, seq, heads, tq=256):
    t = batch * seq
    hd = HEAD_DIM
    branches = tuple((dil, window // (2 * dil)) for window, dil in DIL_BRANCHES)
    spec = lambda part: pl.BlockSpec((seq, hd), lambda b, h: (b, part * heads + h))
    return pl.pallas_call(
        functools.partial(_dilated_attn_body, branches=branches, tq=tq),
        grid=(batch, heads),
        in_specs=[spec(0), spec(1), spec(2)],
        out_specs=spec(0),
        out_shape=jax.ShapeDtypeStruct((t, heads * hd), BF16),
        scratch_shapes=[pltpu.VMEM((seq, hd), F32)] * 3,
        compiler_params=_params(("parallel", "parallel"), seq * hd * (2 * 3 * 4 + 2 * 2 + 3 * 4) + (16 << 20)),
        name="dilated_attn",
    )(qkv, qkv, qkv)


def _router_body(x_ref, g_ref, wr_ref, hn_ref, e_ref, p_ref):
    hn = _rms_rows(x_ref[...], g_ref[...])
    hn_ref[...] = hn
    logits = jnp.dot(hn, wr_ref[...], precision=lax.Precision.HIGHEST, preferred_element_type=F32)
    lane = lax.broadcasted_iota(jnp.int32, logits.shape, 1)
    logits = jnp.where(lane < N_EXPERTS, logits, -jnp.inf)
    m1 = jnp.max(logits, axis=-1, keepdims=True)
    i1 = jnp.min(jnp.where(logits == m1, lane, LANES), axis=-1, keepdims=True)
    rest = jnp.where(lane == i1, -jnp.inf, logits)
    m2 = jnp.max(rest, axis=-1, keepdims=True)
    i2 = jnp.min(jnp.where(rest == m2, lane, LANES), axis=-1, keepdims=True)
    e2 = jnp.exp(m2 - m1)
    den = 1.0 + e2
    e_ref[...] = jnp.where(lane == 0, i1, jnp.where(lane == 1, i2, 0))
    p_ref[...] = jnp.where(lane == 0, 1.0 / den, jnp.where(lane == 1, e2 / den, 0.0))


def moe_router(x, gain, w_router_padded, tm=512):
    m, d = x.shape
    return pl.pallas_call(
        _router_body,
        grid=(m // tm,),
        in_specs=[pl.BlockSpec((tm, d), lambda i: (i, 0)),
                  pl.BlockSpec((1, d), lambda i: (0, 0)),
                  pl.BlockSpec((d, LANES), lambda i: (0, 0))],
        out_specs=[pl.BlockSpec((tm, d), lambda i: (i, 0)),
                   pl.BlockSpec((tm, LANES), lambda i: (i, 0)),
                   pl.BlockSpec((tm, LANES), lambda i: (i, 0))],
        out_shape=[jax.ShapeDtypeStruct((m, d), F32),
                   jax.ShapeDtypeStruct((m, LANES), jnp.int32),
                   jax.ShapeDtypeStruct((m, LANES), F32)],
        compiler_params=_params(("parallel",), 6 * tm * d * 4 + (8 << 20)),
        name="moe_router",
    )(x, gain.reshape(1, d), w_router_padded)


def _expert_body(blk_e_ref, n_used_ref, tok_ref, tok_next_ref, gate_ref, hn_hbm, wg_ref, wu_ref, wd_ref,
                 y_ref, xf_ref, xb_ref, sem, *, nf, nblk):
    b = pl.program_id(0)
    f = pl.program_id(1)
    tb = y_ref.shape[0]
    n_used = n_used_ref[0]
    used = b < n_used
    slot = lax.rem(b, 2)
    per_step = -(-tb // nf)

    def row_copy(toks, j, dst_slot):
        return pltpu.make_async_copy(hn_hbm.at[pl.ds(toks[0, jnp.minimum(j, tb - 1)], 1), :],
                                     xf_ref.at[dst_slot, pl.ds(j, 1), :], sem.at[dst_slot])

    def wait_gather(dst_slot):
        def wait(j, c):
            row_copy(tok_ref, j, dst_slot).wait()
            return c
        lax.fori_loop(0, per_step * nf, wait, 0)

    @pl.when((b == 0) & (f == 0))
    def _():
        def start(j, c):
            row_copy(tok_ref, j, 0).start()
            return c
        lax.fori_loop(0, per_step * nf, start, 0)

    @pl.when(used & (f == 0))
    def _():
        wait_gather(slot)
        xb_ref[...] = xf_ref[slot, :tb, :].astype(BF16)
        y_ref[...] = jnp.zeros_like(y_ref)

    @pl.when(~used & (f == 0))
    def _():
        y_ref[...] = jnp.zeros_like(y_ref)

    @pl.when(used)
    def _():
        for j in range(per_step):
            row_copy(tok_next_ref, f * per_step + j, 1 - slot).start()
        xb = xb_ref[...]
        a = jnp.dot(xb, wg_ref[...], preferred_element_type=F32)
        u = jnp.dot(xb, wu_ref[...], preferred_element_type=F32)
        h = (_silu(a) * u).astype(BF16)
        y_ref[...] += jnp.dot(h, wd_ref[...], preferred_element_type=F32)

    @pl.when(used & (f == nf - 1))
    def _():
        y_ref[...] = y_ref[...] * gate_ref[...]

    @pl.when(used & (f == nf - 1) & (b + 1 >= n_used))
    def _():
        wait_gather(1 - slot)


def moe_experts(hn, slot_tok, slot_gate, blk_e, n_used, wg, wu, wd, tb, tf=512):
    t, d = hn.shape
    p = slot_tok.shape[0]
    nblk = p // tb
    fdim = wg.shape[2]
    nf = fdim // tf

    def live(b, n_used_ref):
        return jnp.minimum(b, n_used_ref[0] - 1)

    def wmap_cols(b, f, blk_e_ref, n_used_ref):
        return (blk_e_ref[live(b, n_used_ref)], 0, jnp.where(b < n_used_ref[0], f, nf - 1))

    def wmap_rows(b, f, blk_e_ref, n_used_ref):
        return (blk_e_ref[live(b, n_used_ref)], jnp.where(b < n_used_ref[0], f, nf - 1), 0)

    vmem = (tb * d * (2 * 4 + 2) + 2 * tb * d * 4 + 2 * (2 * d * tf + tf * d) * 2 + 3 * tb * tf * 4
            + 2 * tb * d * 4 + (4 << 20))
    toks = slot_tok.reshape(nblk, 1, tb)
    grid_spec = pltpu.PrefetchScalarGridSpec(
        num_scalar_prefetch=2,
        grid=(nblk, nf),
        in_specs=[pl.BlockSpec((None, 1, tb), lambda b, f, *_: (b, 0, 0), memory_space=pltpu.SMEM),
                  pl.BlockSpec((None, 1, tb), lambda b, f, *_: (jnp.minimum(b + 1, nblk - 1), 0, 0),
                               memory_space=pltpu.SMEM),
                  pl.BlockSpec((tb, 1), lambda b, f, *_: (b, 0)),
                  pl.BlockSpec(memory_space=pl.ANY),
                  pl.BlockSpec((None, d, tf), wmap_cols),
                  pl.BlockSpec((None, d, tf), wmap_cols),
                  pl.BlockSpec((None, tf, d), wmap_rows)],
        out_specs=pl.BlockSpec((tb, d), lambda b, f, *_: (b, 0)),
        scratch_shapes=[pltpu.VMEM((2, -(-tb // nf) * nf, d), F32), pltpu.VMEM((tb, d), BF16),
                        pltpu.SemaphoreType.DMA((2,))],
    )
    return pl.pallas_call(
        functools.partial(_expert_body, nf=nf, nblk=nblk),
        grid_spec=grid_spec,
        out_shape=jax.ShapeDtypeStruct((p, d), F32),
        compiler_params=_params(("arbitrary", "arbitrary"), vmem),
        name="moe_experts",
    )(blk_e, n_used, toks, toks, slot_gate.reshape(p, 1), hn, wg, wu, wd)


def _combine_body(slot_ref, slot_next_ref, x_ref, y_hbm, o_ref, buf_ref, sem):
    i = pl.program_id(0)
    nt = pl.num_programs(0)
    tm = x_ref.shape[0]
    cur = lax.rem(i, 2)

    def row_copy(slots, j, half):
        k, r = divmod(j, tm)
        return pltpu.make_async_copy(y_hbm.at[pl.ds(slots[0, j], 1), :],
                                     buf_ref.at[half, k, pl.ds(r, 1), :], sem.at[half])

    def wait_tile(half):
        def wait(j, c):
            pltpu.make_async_copy(y_hbm.at[pl.ds(0, 1), :], buf_ref.at[half, 0, pl.ds(0, 1), :],
                                  sem.at[half]).wait()
            return c
        lax.fori_loop(0, TOP_K * tm, wait, 0)

    @pl.when(i == 0)
    def _():
        for j in range(TOP_K * tm):
            row_copy(slot_ref, j, 0).start()

    for j in range(TOP_K * tm):
        row_copy(slot_next_ref, j, 1 - cur).start()
    wait_tile(cur)
    o_ref[...] = x_ref[...] + (buf_ref[cur, 0] + buf_ref[cur, 1])

    @pl.when(i == nt - 1)
    def _():
        wait_tile(1 - cur)


def moe_combine(x, y, tok_slots, tm=256):
    t, d = x.shape
    nt = t // tm
    slots = tok_slots.reshape(nt, tm, TOP_K).transpose(0, 2, 1).reshape(nt, 1, TOP_K * tm)
    slot_spec = lambda index_map: pl.BlockSpec((None, 1, TOP_K * tm), index_map, memory_space=pltpu.SMEM)
    return pl.pallas_call(
        _combine_body,
        grid=(nt,),
        in_specs=[slot_spec(lambda i: (i, 0, 0)),
                  slot_spec(lambda i: (jnp.minimum(i + 1, nt - 1), 0, 0)),
                  pl.BlockSpec((tm, d), lambda i: (i, 0)),
                  pl.BlockSpec(memory_space=pl.ANY)],
        out_specs=pl.BlockSpec((tm, d), lambda i: (i, 0)),
        out_shape=jax.ShapeDtypeStruct((t, d), F32),
        scratch_shapes=[pltpu.VMEM((2, TOP_K, tm, d), F32), pltpu.SemaphoreType.DMA((2,))],
        compiler_params=_params(("arbitrary",), (4 + 2 * TOP_K) * tm * d * 4 + (4 << 20)),
        name="moe_combine",
    )(slots, slots, x, y)


def _routing_plan(top_e, gates, tb):
    t = top_e.shape[0]
    n = t * TOP_K
    e_flat = top_e.reshape(n)
    onehot = (e_flat[:, None] == jnp.arange(N_EXPERTS, dtype=jnp.int32)[None, :]).astype(jnp.int32)
    rank = jnp.take_along_axis(jnp.cumsum(onehot, axis=0) - onehot, e_flat[:, None], axis=1)[:, 0]
    counts = jnp.sum(onehot, axis=0)
    padded = (counts + tb - 1) // tb * tb
    pend = jnp.cumsum(padded)
    dest = (pend - padded)[e_flat] + rank
    nblk = -(-n // tb) + N_EXPERTS
    p = nblk * tb
    tok_flat = jnp.arange(n, dtype=jnp.int32) // TOP_K
    slot_tok = jnp.zeros((p,), jnp.int32).at[dest].set(tok_flat)
    slot_gate = jnp.zeros((p,), F32).at[dest].set(gates.reshape(n))
    blk_start = jnp.arange(nblk, dtype=pend.dtype) * tb
    blk_e = jnp.minimum(jnp.searchsorted(pend, blk_start, side='right'), N_EXPERTS - 1).astype(jnp.int32)
    n_used = (pend[-1] // tb).astype(jnp.int32).reshape(1)
    return slot_tok, slot_gate, blk_e, n_used, dest.reshape(t, TOP_K).astype(jnp.int32)


def _rope_tables(seq, hd):
    half = hd // 2
    inv_freq = ROPE_THETA ** (-jnp.arange(half, dtype=F32) * 2.0 / hd)
    ang = jnp.arange(seq, dtype=F32)[:, None] * inv_freq[None, :]
    cos, sin = jnp.cos(ang), jnp.sin(ang)
    return jnp.concatenate([cos, cos], axis=-1), jnp.concatenate([-sin, sin], axis=-1)


def kernel(x, norm_gains, hgrn_w_in, hgrn_lb_logits, hgrn_onorm, hgrn_w_out, attn_w_qkv, attn_q_gain,
           attn_k_gain, attn_w_out, ffn_w_gate, ffn_w_up, ffn_w_down, moe_w_router, moe_w_gate, moe_w_up,
           moe_w_down):
    batch, seq, d = x.shape
    t = batch * seq
    heads = d // HEAD_DIM
    hd = HEAD_DIM
    xf = x.reshape(t, d)

    w_in = hgrn_w_in[0]
    w_qvg = jnp.concatenate([w_in[:, :d], w_in[:, 3 * d:5 * d]], axis=1).astype(BF16)
    w_f = w_in[:, d:3 * d].astype(BF16)
    lb = jnp.cumsum(jax.nn.softmax(hgrn_lb_logits.astype(F32), axis=0), axis=0)[0].reshape(1, 2 * d)
    tn = 512
    nq = d // tn

    def qvg_epilogue(acc, j, aux, outs):
        is_v = (j >= nq) & (j < 2 * nq)
        outs[0][...] = jnp.where(is_v, acc, _silu(acc)).astype(BF16)

    (qvg,) = norm_matmul(
        "hgrn_qvg_proj", xf, norm_gains[0, 0], w_qvg, qvg_epilogue,
        [jax.ShapeDtypeStruct((t, 3 * d), BF16)], [pl.BlockSpec((1024, tn), lambda i, j: (i, j))], tn=tn)

    def f_epilogue(acc, j, aux, outs):
        lbv = aux[0][...]
        fgate = lbv + (1.0 - lbv) * jax.nn.sigmoid(acc)
        outs[0][...] = jnp.log2(fgate)
        outs[1][...] = (1.0 - fgate).astype(BF16)

    log2f, kk = norm_matmul(
        "hgrn_forget_proj", xf, norm_gains[0, 0], w_f, f_epilogue,
        [jax.ShapeDtypeStruct((t, 2 * d), F32), jax.ShapeDtypeStruct((t, 2 * d), BF16)],
        [pl.BlockSpec((1024, tn), lambda i, j: (i, j)), pl.BlockSpec((1024, tn), lambda i, j: (i, j))],
        aux=(lb,), aux_specs=(pl.BlockSpec((1, tn), lambda i, j: (0, j)),), tn=tn)

    o_f, o_b = gla_bidirectional(qvg, kk, log2f, batch=batch, seq=seq, heads=heads)

    tm = 512
    x1 = proj_residual(
        "hgrn_out_proj", (o_f, o_b, qvg, hgrn_onorm[0].reshape(1, d)),
        (pl.BlockSpec((tm, d), lambda i, j: (i, 0)), pl.BlockSpec((tm, d), lambda i, j: (i, 0)),
         pl.BlockSpec((tm, d), lambda i, j: (i, 2)), pl.BlockSpec((1, d), lambda i, j: (0, 0))),
        _hgrn_out_prologue, hgrn_w_out[0].astype(BF16), xf, tm=tm)

    fdim = ffn_w_gate.shape[2]
    fpad = -(-fdim // 512) * 512 - fdim
    wg = jnp.pad(ffn_w_gate[0], ((0, 0), (0, fpad))).astype(BF16)
    wu = jnp.pad(ffn_w_up[0], ((0, 0), (0, fpad))).astype(BF16)
    wd = jnp.pad(ffn_w_down[0], ((0, fpad), (0, 0))).astype(BF16)
    x2 = norm_swiglu_residual(x1, norm_gains[0, 1], wg, wu, wd)

    cos, sin = _rope_tables(seq, hd)
    qg = attn_q_gain[0].reshape(1, hd)
    kg = attn_k_gain[0].reshape(1, hd)
    tm_qkv = 1024
    pos_blocks = seq // tm_qkv

    def qkv_epilogue(acc, j, aux, outs):
        cos_ref, sin_ref, qg_ref, kg_ref = aux
        o_ref = outs[0]

        def normed_rope(gain, scale):
            for h in range(tn // hd):
                hs = slice(h * hd, (h + 1) * hd)
                y = _rms_rows(acc[:, hs], gain)
                y = y * cos_ref[...] + pltpu.roll(y, hd // 2, 1) * sin_ref[...]
                o_ref[:, hs] = y * scale

        @pl.when(j < nq)
        def _():
            normed_rope(qg_ref[...], hd ** -0.5)

        @pl.when((j >= nq) & (j < 2 * nq))
        def _():
            normed_rope(kg_ref[...], 1.0)

        @pl.when(j >= 2 * nq)
        def _():
            o_ref[...] = acc

    (qkv,) = norm_matmul(
        "attn_qkv_proj", x2, norm_gains[1, 0], attn_w_qkv[0].astype(BF16), qkv_epilogue,
        [jax.ShapeDtypeStruct((t, 3 * d), F32)], [pl.BlockSpec((tm_qkv, tn), lambda i, j: (i, j))],
        aux=(cos, sin, qg, kg),
        aux_specs=(pl.BlockSpec((tm_qkv, hd), lambda i, j: (i % pos_blocks, 0)),
                   pl.BlockSpec((tm_qkv, hd), lambda i, j: (i % pos_blocks, 0)),
                   pl.BlockSpec((1, hd), lambda i, j: (0, 0)),
                   pl.BlockSpec((1, hd), lambda i, j: (0, 0))),
        tm=tm_qkv, tn=tn)

    attn = dilated_attention(qkv, batch=batch, seq=seq, heads=heads)
    x3 = proj_residual("attn_out_proj", (attn,), (pl.BlockSpec((tm, d), lambda i, j: (i, 0)),),
                       _copy_prologue, attn_w_out[0].astype(BF16), x2, tm=tm)

    tb = 512
    w_router = jnp.pad(moe_w_router[0].astype(F32), ((0, 0), (0, LANES - N_EXPERTS)))
    hn3, e_pad, p_pad = moe_router(x3, norm_gains[1, 1], w_router)
    slot_tok, slot_gate, blk_e, n_used, tok_slots = _routing_plan(e_pad[:, :TOP_K], p_pad[:, :TOP_K], tb)
    y = moe_experts(hn3, slot_tok, slot_gate, blk_e, n_used, moe_w_gate[0].astype(BF16),
                    moe_w_up[0].astype(BF16), moe_w_down[0].astype(BF16), tb)
    out = moe_combine(x3, y, tok_slots)
    return out.reshape(batch, seq, d)
```

```python
import functools

import jax
import jax.numpy as jnp
from jax import lax
from jax.experimental import pallas as pl
from jax.experimental.pallas import tpu as pltpu

F32 = jnp.float32
BF16 = jnp.bfloat16
EPS = 1e-6
NEG_INF = -1e30
ROPE_THETA = 10000.0

HEAD_DIM = 128
GLA_CHUNK = 64
DIL_BRANCHES = ((128, 1), (512, 4), (2048, 16))
N_EXPERTS = 8
TOP_K = 2

LANES = 128
V7X_VMEM_BYTES = 64 * 1024 * 1024
VMEM_BUDGET = 56 * 1024 * 1024

_NT = (((1,), (1,)), ((), ()))
_TN = (((0,), (0,)), ((), ()))


def _params(semantics, vmem_bytes):
    return pltpu.CompilerParams(dimension_semantics=semantics,
                                vmem_limit_bytes=int(min(vmem_bytes, VMEM_BUDGET)))


def _silu(x):
    return x * jax.nn.sigmoid(x)


def _rms_rows(x, gain):
    ms = jnp.mean(x * x, axis=-1, keepdims=True)
    return x * lax.rsqrt(ms + EPS) * gain


def _norm_matmul_body(x_ref, g_ref, w_ref, *rest, n_aux, epilogue, sub):
    aux, outs, hn_ref = rest[:n_aux], rest[n_aux:-1], rest[-1]
    j = pl.program_id(1)

    @pl.when(j == 0)
    def _():
        hn_ref[...] = _rms_rows(x_ref[...], g_ref[...]).astype(BF16)

    for s in range(w_ref.shape[1] // sub):
        cols = slice(s * sub, (s + 1) * sub)
        acc = jnp.dot(hn_ref[...], w_ref[:, cols], preferred_element_type=F32)
        epilogue(acc, j, aux, outs, cols)


def norm_matmul(name, x, gain, w, epilogue, out_shapes, out_specs, aux=(), aux_specs=(), tm=1024, tn=512,
                sub=256):
    m, d = x.shape
    n = w.shape[1]
    assert m % tm == 0 and n % tn == 0
    out_bytes = sum(2 * tm * tn * jnp.dtype(s.dtype).itemsize for s in out_shapes)
    vmem = 2 * tm * d * 4 + tm * d * 2 + 2 * d * tn * 2 + out_bytes + 4 * tm * tn * 4 + (4 << 20)
    return pl.pallas_call(
        functools.partial(_norm_matmul_body, n_aux=len(aux), epilogue=epilogue, sub=min(sub, tn)),
        grid=(m // tm, n // tn),
        in_specs=[pl.BlockSpec((tm, d), lambda i, j: (i, 0)),
                  pl.BlockSpec((1, d), lambda i, j: (0, 0)),
                  pl.BlockSpec((d, tn), lambda i, j: (0, j)),
                  *aux_specs],
        out_specs=out_specs,
        out_shape=out_shapes,
        scratch_shapes=[pltpu.VMEM((tm, d), BF16)],
        compiler_params=_params(("parallel", "arbitrary"), vmem),
        name=name,
    )(x, gain.reshape(1, d), w, *aux)


def _gla_chunks(chains):
    c = chains[0]["q"].shape[0]
    n_levels = c.bit_length()
    for ch in chains:
        hi = ch["lf2"].astype(BF16)
        lo = (ch["lf2"] - hi.astype(F32)).astype(BF16)
        ch["s"] = jnp.dot(ch["sums"], jnp.concatenate([hi, lo], axis=0), preferred_element_type=F32)
    for ch in chains:
        qd = ch["q"] * jnp.exp2(ch["s"][:c]).astype(BF16)
        ch["o"] = lax.dot_general(qd, ch["st"].astype(BF16), _NT, preferred_element_type=F32)
        ch["a"] = jnp.zeros((c, 2 * c), F32)
    for level in range(0, n_levels, 2):
        for ch in chains:
            qs, ks = [], []
            for l in (level, level + 1):
                if l == 0:
                    qs.append(ch["q"])
                    ks.append(ch["k"])
                elif l < n_levels:
                    e = jnp.exp2(-jnp.abs(ch["s"][l * c:(l + 1) * c])).astype(BF16)
                    qs.append(ch["q"] * e)
                    ks.append(ch["k"] * e)
            if len(ks) == 1:
                ks = ks * 2
            p = lax.dot_general(jnp.concatenate(qs, axis=0), jnp.concatenate(ks, axis=0), _NT,
                                preferred_element_type=F32)
            for i in range(len(qs)):
                ch["a"] = jnp.where(ch["lvl"] == 2 * (level + i) + i, p[i * c:(i + 1) * c, :], ch["a"])
    outs = []
    for ch in chains:
        g = ch["s"][:c]
        g_tot = g[c - 1:c, :] if ch["fwd"] else g[0:1, :]
        vv = jnp.concatenate([ch["v"], ch["v"]], axis=0)
        o = ch["o"] + jnp.dot(ch["a"].astype(BF16), vv, preferred_element_type=F32)
        kd = ch["k"] * jnp.exp2(g_tot - g).astype(BF16)
        st_new = (ch["st"] * jnp.exp2(g_tot)
                  + lax.dot_general(ch["v"], kd, _TN, preferred_element_type=F32))
        outs.append((o, st_new))
    return outs


def _gla_tables(c, fwd):
    ti = lax.broadcasted_iota(jnp.int32, (c, c), 0)
    ui = lax.broadcasted_iota(jnp.int32, (c, c), 1)

    def cum(row):
        return ((ui <= row) if fwd else (ui >= row)).astype(jnp.int32)

    blocks = [cum(ti)]
    half = 1
    while half < c:
        boundary = (ti & -(2 * half)) + (half - 1 if fwd else half)
        blocks.append(cum(ti) - cum(boundary))
        half *= 2
    x = ti ^ ui
    top_bit = sum((x >= (1 << b)).astype(jnp.int32) for b in range(1, c.bit_length() - 1))
    lvl = jnp.where(ti == ui, 0, jnp.where((ui < ti) if fwd else (ui > ti), 1 + top_bit, -1))
    sums = jnp.concatenate(blocks, axis=0).astype(F32).astype(BF16)
    codes = jnp.concatenate([2 * lvl, 2 * lvl + 1], axis=1)
    return jnp.concatenate([sums, sums], axis=1), codes


def _gla_body(qf_ref, kf_ref, vf_ref, lf_ref, qb_ref, kb_ref, vb_ref, lb_ref, of_ref, ob_ref,
              st_ref, sums_ref, lvl_ref, *, chunk):
    rows, width = qf_ref.shape
    nch = rows // chunk
    c = chunk

    @pl.when(pl.program_id(2) == 0)
    def _():
        st_ref[...] = jnp.zeros_like(st_ref)

    dirs = []
    for d, (fwd, refs) in enumerate(((True, (qf_ref, kf_ref, vf_ref, lf_ref, of_ref)),
                                     (False, (qb_ref, kb_ref, vb_ref, lb_ref, ob_ref)))):
        sums, lvl = _gla_tables(c, fwd)
        sums_ref[d] = sums
        lvl_ref[d] = lvl
        dirs.append((fwd, refs))

    def one_chunk(ci, carry):
        chains, dests = [], []
        for d, (fwd, (q_ref, k_ref, v_ref, l_ref, o_ref)) in enumerate(dirs):
            r0 = pl.multiple_of((ci if fwd else nch - 1 - ci) * c, c)
            for h in range(width // HEAD_DIM):
                hs = slice(h * HEAD_DIM, (h + 1) * HEAD_DIM)
                chains.append(dict(q=q_ref[pl.ds(r0, c), hs], k=k_ref[pl.ds(r0, c), hs],
                                   v=v_ref[pl.ds(r0, c), hs], lf2=l_ref[pl.ds(r0, c), hs],
                                   st=st_ref[d, h], lvl=lvl_ref[d], fwd=fwd, sums=sums_ref[d]))
                dests.append((o_ref, r0, hs, d, h))
        for (o, st_new), (o_ref, r0, hs, d, h) in zip(_gla_chunks(chains), dests):
            o_ref[pl.ds(r0, c), hs] = o.astype(o_ref.dtype)
            st_ref[d, h] = st_new
        return carry

    lax.fori_loop(0, nch, one_chunk, 0)


def gla_bidirectional(qvg, kk, log2f, *, batch, seq, heads, heads_per_step=4, rows_per_step=512):
    t = batch * seq
    hg = heads_per_step
    width = hg * HEAD_DIM
    rb = rows_per_step
    ns = seq // rb
    ng = heads // hg

    def spec(fwd, seg):
        return pl.BlockSpec((rb, width),
                            lambda b, g, s: (b * ns + (s if fwd else ns - 1 - s), seg * ng + g))

    return pl.pallas_call(
        functools.partial(_gla_body, chunk=GLA_CHUNK),
        grid=(batch, ng, ns),
        in_specs=[spec(True, 0), spec(True, 0), spec(True, 1), spec(True, 0),
                  spec(False, 0), spec(False, 1), spec(False, 1), spec(False, 1)],
        out_specs=[spec(True, 0), spec(False, 0)],
        out_shape=[jax.ShapeDtypeStruct((t, heads * HEAD_DIM), BF16)] * 2,
        scratch_shapes=[pltpu.VMEM((2, hg, HEAD_DIM, HEAD_DIM), F32),
                        pltpu.VMEM((2, GLA_CHUNK * GLA_CHUNK.bit_length(), 2 * GLA_CHUNK), BF16),
                        pltpu.VMEM((2, GLA_CHUNK, 2 * GLA_CHUNK), jnp.int32)],
        compiler_params=_params(("parallel", "parallel", "arbitrary"), 32 << 20),
        name="gla_bidir",
    )(qvg, kk, qvg, log2f, qvg, kk, qvg, log2f)


def _proj_residual_body(*refs, n_pro, prologue):
    pro, (w_ref, x_ref, o_ref, y_ref) = refs[:n_pro], refs[n_pro:]

    @pl.when(pl.program_id(1) == 0)
    def _():
        prologue(pro, y_ref)

    o_ref[...] = x_ref[...] + jnp.dot(y_ref[...], w_ref[...], preferred_element_type=F32)


def proj_residual(name, pro_inputs, pro_specs, prologue, w, xres, tm=512, tn=512):
    m, n = xres.shape
    kdim = w.shape[0]
    pro_bytes = sum(2 * tm * kdim * jnp.dtype(a.dtype).itemsize for a in pro_inputs)
    vmem = pro_bytes + tm * kdim * 2 + 2 * kdim * tn * 2 + 4 * tm * tn * 4 + 4 * tm * kdim * 4 + (4 << 20)
    return pl.pallas_call(
        functools.partial(_proj_residual_body, n_pro=len(pro_inputs), prologue=prologue),
        grid=(m // tm, n // tn),
        in_specs=[*pro_specs,
                  pl.BlockSpec((kdim, tn), lambda i, j: (0, j)),
                  pl.BlockSpec((tm, tn), lambda i, j: (i, j))],
        out_specs=pl.BlockSpec((tm, tn), lambda i, j: (i, j)),
        out_shape=jax.ShapeDtypeStruct((m, n), F32),
        scratch_shapes=[pltpu.VMEM((tm, kdim), BF16)],
        compiler_params=_params(("parallel", "arbitrary"), vmem),
        name=name,
    )(*pro_inputs, w, xres)


def _hgrn_out_prologue(pro, y_ref):
    of_ref, ob_ref, gate_ref, gain_ref = pro
    for h in range(of_ref.shape[1] // HEAD_DIM):
        hs = slice(h * HEAD_DIM, (h + 1) * HEAD_DIM)
        o = of_ref[:, hs].astype(F32) + ob_ref[:, hs].astype(F32)
        y = _rms_rows(o, gain_ref[:, hs]) * gate_ref[:, hs].astype(F32)
        y_ref[:, hs] = y.astype(BF16)


def _copy_prologue(pro, y_ref):
    y_ref[...] = pro[0][...]


def _swiglu_body(x_ref, g_ref, wg_ref, wu_ref, wd_ref, o_ref, hn_ref):
    @pl.when(pl.program_id(1) == 0)
    def _():
        x = x_ref[...]
        hn_ref[...] = _rms_rows(x, g_ref[...]).astype(BF16)
        o_ref[...] = x

    hn = hn_ref[...]
    a = jnp.dot(hn, wg_ref[...], preferred_element_type=F32)
    u = jnp.dot(hn, wu_ref[...], preferred_element_type=F32)
    h = (_silu(a) * u).astype(BF16)
    o_ref[...] += jnp.dot(h, wd_ref[...], preferred_element_type=F32)


def norm_swiglu_residual(x, gain, wg, wu, wd, tm=512, tf=512):
    m, d = x.shape
    f = wg.shape[1]
    assert m % tm == 0 and f % tf == 0
    vmem = 4 * tm * d * 4 + tm * d * 2 + 2 * (2 * d * tf + tf * d) * 2 + 3 * tm * tf * 4 + 2 * tm * d * 4 + (4 << 20)
    return pl.pallas_call(
        _swiglu_body,
        grid=(m // tm, f // tf),
        in_specs=[pl.BlockSpec((tm, d), lambda i, j: (i, 0)),
                  pl.BlockSpec((1, d), lambda i, j: (0, 0)),
                  pl.BlockSpec((d, tf), lambda i, j: (0, j)),
                  pl.BlockSpec((d, tf), lambda i, j: (0, j)),
                  pl.BlockSpec((tf, d), lambda i, j: (j, 0))],
        out_specs=pl.BlockSpec((tm, d), lambda i, j: (i, 0)),
        out_shape=jax.ShapeDtypeStruct((m, d), F32),
        scratch_shapes=[pltpu.VMEM((tm, d), BF16)],
        compiler_params=_params(("parallel", "arbitrary"), vmem),
        name="dense_swiglu",
    )(x, gain.reshape(1, d), wg, wu, wd)


def _dilated_attn_body(*refs, hg, branches, tq):
    q_refs, k_refs, v_refs = (refs[i * hg:(i + 1) * hg] for i in range(3))
    o_ref = refs[3 * hg]
    acc_refs, m_refs, l_refs = (refs[3 * hg + 1 + i * hg:3 * hg + 1 + (i + 1) * hg] for i in range(3))
    bias_ref = refs[6 * hg + 1]
    seq = q_refs[0].shape[0]
    steps = branches[0][1]
    tq_max = min(tq, seq)
    wk_max = min(seq, tq_max + 2 * steps)

    d = (lax.broadcasted_iota(jnp.int32, (tq_max, wk_max), 1)
         - lax.broadcasted_iota(jnp.int32, (tq_max, wk_max), 0))
    for shift in range(3):
        bias_ref[shift] = jnp.where(jnp.abs(d - shift * steps) <= steps, 0.0, NEG_INF)

    for bi, (r, br_steps) in enumerate(branches):
        assert br_steps == steps
        n = seq // r
        tqb = min(tq, n)
        wk = min(n, tqb + 2 * steps)
        nqb = n // tqb

        def block(idx, carry, bi=bi, r=r, n=n, tqb=tqb, wk=wk, nqb=nqb):
            rho = idx // nqb
            q0 = (idx - rho * nqb) * tqb
            k0 = jnp.clip(q0 - steps, 0, n - wk)
            bias = bias_ref[(q0 - k0) // steps, :tqb, :wk]

            def rows(c0, cnt):
                if r == 1:
                    return pl.ds(pl.multiple_of(c0, steps), cnt)
                return pl.ds(rho + r * c0, cnt, stride=r)

            s = [lax.dot_general(q_refs[h][rows(q0, tqb), :].astype(BF16),
                                 k_refs[h][rows(k0, wk), :].astype(BF16), _NT,
                                 preferred_element_type=F32) + bias for h in range(hg)]
            m_new = [jnp.broadcast_to(jnp.max(s[h], axis=-1, keepdims=True), (tqb, HEAD_DIM))
                     for h in range(hg)]
            p = [jnp.exp(s[h] - m_new[h][:, :1]) for h in range(hg)]
            l_new = [jnp.broadcast_to(jnp.sum(p[h], axis=-1, keepdims=True), (tqb, HEAD_DIM))
                     for h in range(hg)]
            acc_new = [jnp.dot(p[h].astype(BF16), v_refs[h][rows(k0, wk), :].astype(BF16),
                               preferred_element_type=F32) for h in range(hg)]
            for h in range(hg):
                if bi > 0:
                    m_old = m_refs[h][rows(q0, tqb), :]
                    m_all = jnp.maximum(m_old, m_new[h])
                    w_old = jnp.exp(m_old - m_all)
                    w_new = jnp.exp(m_new[h] - m_all)
                    acc_new[h] = acc_refs[h][rows(q0, tqb), :] * w_old + acc_new[h] * w_new
                    l_new[h] = l_refs[h][rows(q0, tqb), :] * w_old + l_new[h] * w_new
                    m_new[h] = m_all
                acc_refs[h][rows(q0, tqb), :] = acc_new[h]
                l_refs[h][rows(q0, tqb), :] = l_new[h]
                m_refs[h][rows(q0, tqb), :] = m_new[h]
            return carry

        lax.fori_loop(0, r * nqb, block, 0)

    for h in range(hg):
        o_ref[:, h * HEAD_DIM:(h + 1) * HEAD_DIM] = (acc_refs[h][...] / l_refs[h][...]).astype(o_ref.dtype)


def dilated_attention(qkv, *, batch, seq, heads, heads_per_step=2, tq=256):
    t = batch * seq
    hd = HEAD_DIM
    hg = heads_per_step
    branches = tuple((dil, window // (2 * dil)) for window, dil in DIL_BRANCHES)
    steps = branches[0][1]
    tq_max = min(tq, seq)
    specs = lambda part: [pl.BlockSpec((seq, hd), lambda b, g, h=h: (b, part * heads + g * hg + h))
                          for h in range(hg)]
    return pl.pallas_call(
        functools.partial(_dilated_attn_body, hg=hg, branches=branches, tq=tq),
        grid=(batch, heads // hg),
        in_specs=specs(0) + specs(1) + specs(2),
        out_specs=pl.BlockSpec((seq, hg * hd), lambda b, g: (b, g)),
        out_shape=jax.ShapeDtypeStruct((t, heads * hd), BF16),
        scratch_shapes=([pltpu.VMEM((seq, hd), F32)] * (3 * hg)
                        + [pltpu.VMEM((3, tq_max, min(seq, tq_max + 2 * steps)), F32)]),
        compiler_params=_params(("parallel", "parallel"),
                                hg * seq * hd * (2 * 3 * 4 + 2 * 2 + 3 * 4) + (16 << 20)),
        name="dilated_attn",
    )(*([qkv] * (3 * hg)))


def _router_body(x_ref, g_ref, wr_ref, hn_ref, e_ref, p_ref):
    hn = _rms_rows(x_ref[...], g_ref[...])
    hn_ref[...] = hn
    logits = jnp.dot(hn, wr_ref[...], precision=lax.Precision.HIGHEST, preferred_element_type=F32)
    lane = lax.broadcasted_iota(jnp.int32, logits.shape, 1)
    logits = jnp.where(lane < N_EXPERTS, logits, -jnp.inf)
    m1 = jnp.max(logits, axis=-1, keepdims=True)
    i1 = jnp.min(jnp.where(logits == m1, lane, LANES), axis=-1, keepdims=True)
    rest = jnp.where(lane == i1, -jnp.inf, logits)
    m2 = jnp.max(rest, axis=-1, keepdims=True)
    i2 = jnp.min(jnp.where(rest == m2, lane, LANES), axis=-1, keepdims=True)
    e2 = jnp.exp(m2 - m1)
    den = 1.0 + e2
    e_ref[...] = jnp.where(lane == 0, i1, jnp.where(lane == 1, i2, 0))
    p_ref[...] = jnp.where(lane == 0, 1.0 / den, jnp.where(lane == 1, e2 / den, 0.0))


def moe_router(x, gain, w_router_padded, tm=512):
    m, d = x.shape
    return pl.pallas_call(
        _router_body,
        grid=(m // tm,),
        in_specs=[pl.BlockSpec((tm, d), lambda i: (i, 0)),
                  pl.BlockSpec((1, d), lambda i: (0, 0)),
                  pl.BlockSpec((d, LANES), lambda i: (0, 0))],
        out_specs=[pl.BlockSpec((tm, d), lambda i: (i, 0)),
                   pl.BlockSpec((tm, LANES), lambda i: (i, 0)),
                   pl.BlockSpec((tm, LANES), lambda i: (i, 0))],
        out_shape=[jax.ShapeDtypeStruct((m, d), F32),
                   jax.ShapeDtypeStruct((m, LANES), jnp.int32),
                   jax.ShapeDtypeStruct((m, LANES), F32)],
        compiler_params=_params(("parallel",), 6 * tm * d * 4 + (8 << 20)),
        name="moe_router",
    )(x, gain.reshape(1, d), w_router_padded)


def _expert_body(blk_e_ref, n_used_ref, tok_ref, tok_next_ref, hn_hbm, wg_ref, wu_ref, wd_ref,
                 y_ref, xf_ref, xb_ref, sem, *, nf, nblk):
    b = pl.program_id(0)
    f = pl.program_id(1)
    tb = y_ref.shape[0]
    n_used = n_used_ref[0]
    used = b < n_used
    slot = lax.rem(b, 2)
    per_step = -(-tb // nf)

    def row_copy(toks, j, dst_slot):
        return pltpu.make_async_copy(hn_hbm.at[pl.ds(toks[0, jnp.minimum(j, tb - 1)], 1), :],
                                     xf_ref.at[dst_slot, pl.ds(j, 1), :], sem.at[dst_slot])

    def wait_gather(dst_slot):
        def wait(j, c):
            row_copy(tok_ref, j, dst_slot).wait()
            return c
        lax.fori_loop(0, per_step * nf, wait, 0)

    @pl.when((b == 0) & (f == 0))
    def _():
        def start(j, c):
            row_copy(tok_ref, j, 0).start()
            return c
        lax.fori_loop(0, per_step * nf, start, 0)

    @pl.when(used & (f == 0))
    def _():
        wait_gather(slot)
        xb_ref[...] = xf_ref[slot, :tb, :].astype(BF16)
        y_ref[...] = jnp.zeros_like(y_ref)

    @pl.when(~used & (f == 0))
    def _():
        y_ref[...] = jnp.zeros_like(y_ref)

    @pl.when(used)
    def _():
        for j in range(per_step):
            row_copy(tok_next_ref, f * per_step + j, 1 - slot).start()
        xb = xb_ref[...]
        a = jnp.dot(xb, wg_ref[...], preferred_element_type=F32)
        u = jnp.dot(xb, wu_ref[...], preferred_element_type=F32)
        h = (_silu(a) * u).astype(BF16)
        y_ref[...] += jnp.dot(h, wd_ref[...], preferred_element_type=F32)

    @pl.when(used & (f == nf - 1) & (b + 1 >= n_used))
    def _():
        wait_gather(1 - slot)


def moe_experts(hn, slot_tok, blk_e, n_used, wg, wu, wd, tb, tf=512):
    t, d = hn.shape
    p = slot_tok.shape[0]
    nblk = p // tb
    fdim = wg.shape[2]
    nf = fdim // tf

    def live(b, n_used_ref):
        return jnp.minimum(b, n_used_ref[0] - 1)

    def wmap_cols(b, f, blk_e_ref, n_used_ref):
        return (blk_e_ref[live(b, n_used_ref)], 0, jnp.where(b < n_used_ref[0], f, nf - 1))

    def wmap_rows(b, f, blk_e_ref, n_used_ref):
        return (blk_e_ref[live(b, n_used_ref)], jnp.where(b < n_used_ref[0], f, nf - 1), 0)

    vmem = (tb * d * (2 * 4 + 2) + 2 * tb * d * 4 + 2 * (2 * d * tf + tf * d) * 2 + 3 * tb * tf * 4
            + 2 * tb * d * 4 + (4 << 20))
    toks = slot_tok.reshape(nblk, 1, tb)
    grid_spec = pltpu.PrefetchScalarGridSpec(
        num_scalar_prefetch=2,
        grid=(nblk, nf),
        in_specs=[pl.BlockSpec((None, 1, tb), lambda b, f, *_: (b, 0, 0), memory_space=pltpu.SMEM),
                  pl.BlockSpec((None, 1, tb), lambda b, f, *_: (jnp.minimum(b + 1, nblk - 1), 0, 0),
                               memory_space=pltpu.SMEM),
                  pl.BlockSpec(memory_space=pl.ANY),
                  pl.BlockSpec((None, d, tf), wmap_cols),
                  pl.BlockSpec((None, d, tf), wmap_cols),
                  pl.BlockSpec((None, tf, d), wmap_rows)],
        out_specs=pl.BlockSpec((tb, d), lambda b, f, *_: (b, 0)),
        scratch_shapes=[pltpu.VMEM((2, -(-tb // nf) * nf, d), F32), pltpu.VMEM((tb, d), BF16),
                        pltpu.SemaphoreType.DMA((2,))],
    )
    return pl.pallas_call(
        functools.partial(_expert_body, nf=nf, nblk=nblk),
        grid_spec=grid_spec,
        out_shape=jax.ShapeDtypeStruct((p, d), F32),
        compiler_params=_params(("arbitrary", "arbitrary"), vmem),
        name="moe_experts",
    )(blk_e, n_used, toks, toks, hn, wg, wu, wd)


def _combine_body(slot_ref, slot_next_ref, x_ref, p_ref, y_hbm, o_ref, buf_ref, sem):
    i = pl.program_id(0)
    nt = pl.num_programs(0)
    tm = x_ref.shape[0]
    cur = lax.rem(i, 2)

    def row_copy(slots, j, half):
        k, r = divmod(j, tm)
        return pltpu.make_async_copy(y_hbm.at[pl.ds(slots[0, j], 1), :],
                                     buf_ref.at[half, k, pl.ds(r, 1), :], sem.at[half])

    def wait_tile(half):
        def wait(j, c):
            pltpu.make_async_copy(y_hbm.at[pl.ds(0, 1), :], buf_ref.at[half, 0, pl.ds(0, 1), :],
                                  sem.at[half]).wait()
            return c
        lax.fori_loop(0, TOP_K * tm, wait, 0)

    @pl.when(i == 0)
    def _():
        for j in range(TOP_K * tm):
            row_copy(slot_ref, j, 0).start()

    for j in range(TOP_K * tm):
        row_copy(slot_next_ref, j, 1 - cur).start()
    wait_tile(cur)
    gates = p_ref[...]
    o_ref[...] = x_ref[...] + (gates[:, 0:1] * buf_ref[cur, 0] + gates[:, 1:2] * buf_ref[cur, 1])

    @pl.when(i == nt - 1)
    def _():
        wait_tile(1 - cur)


def moe_combine(x, y, tok_slots, gates, tm=256):
    t, d = x.shape
    nt = t // tm
    slots = tok_slots.reshape(nt, tm, TOP_K).transpose(0, 2, 1).reshape(nt, 1, TOP_K * tm)
    slot_spec = lambda index_map: pl.BlockSpec((None, 1, TOP_K * tm), index_map, memory_space=pltpu.SMEM)
    return pl.pallas_call(
        _combine_body,
        grid=(nt,),
        in_specs=[slot_spec(lambda i: (i, 0, 0)),
                  slot_spec(lambda i: (jnp.minimum(i + 1, nt - 1), 0, 0)),
                  pl.BlockSpec((tm, d), lambda i: (i, 0)),
                  pl.BlockSpec((tm, LANES), lambda i: (i, 0)),
                  pl.BlockSpec(memory_space=pl.ANY)],
        out_specs=pl.BlockSpec((tm, d), lambda i: (i, 0)),
        out_shape=jax.ShapeDtypeStruct((t, d), F32),
        scratch_shapes=[pltpu.VMEM((2, TOP_K, tm, d), F32), pltpu.SemaphoreType.DMA((2,))],
        compiler_params=_params(("arbitrary",), (4 + 2 * TOP_K) * tm * d * 4 + (4 << 20)),
        name="moe_combine",
    )(slots, slots, x, gates, y)


def _routing_plan(top_e, tb):
    t = top_e.shape[0]
    n = t * TOP_K
    e_flat = top_e.reshape(n)
    onehot = (e_flat[:, None] == jnp.arange(N_EXPERTS, dtype=jnp.int32)[None, :]).astype(jnp.int32)
    rank = jnp.take_along_axis(jnp.cumsum(onehot, axis=0) - onehot, e_flat[:, None], axis=1)[:, 0]
    counts = jnp.sum(onehot, axis=0)
    padded = (counts + tb - 1) // tb * tb
    pend = jnp.cumsum(padded)
    dest = (pend - padded)[e_flat] + rank
    nblk = -(-n // tb) + N_EXPERTS
    p = nblk * tb
    tok_flat = jnp.arange(n, dtype=jnp.int32) // TOP_K
    slot_tok = jnp.zeros((p,), jnp.int32).at[dest].set(tok_flat)
    blk_start = jnp.arange(nblk, dtype=pend.dtype) * tb
    blk_e = jnp.minimum(jnp.searchsorted(pend, blk_start, side='right'), N_EXPERTS - 1).astype(jnp.int32)
    n_used = (pend[-1] // tb).astype(jnp.int32).reshape(1)
    return slot_tok, blk_e, n_used, dest.reshape(t, TOP_K).astype(jnp.int32)


def _rope_tables(seq, hd):
    half = hd // 2
    inv_freq = ROPE_THETA ** (-jnp.arange(half, dtype=F32) * 2.0 / hd)
    ang = jnp.arange(seq, dtype=F32)[:, None] * inv_freq[None, :]
    cos, sin = jnp.cos(ang), jnp.sin(ang)
    return jnp.concatenate([cos, cos], axis=-1), jnp.concatenate([-sin, sin], axis=-1)


def kernel(x, norm_gains, hgrn_w_in, hgrn_lb_logits, hgrn_onorm, hgrn_w_out, attn_w_qkv, attn_q_gain,
           attn_k_gain, attn_w_out, ffn_w_gate, ffn_w_up, ffn_w_down, moe_w_router, moe_w_gate, moe_w_up,
           moe_w_down):
    batch, seq, d = x.shape
    t = batch * seq
    heads = d // HEAD_DIM
    hd = HEAD_DIM
    xf = x.reshape(t, d)

    w_in = hgrn_w_in[0]
    w_qvg = jnp.concatenate([w_in[:, :d], w_in[:, 3 * d:5 * d]], axis=1).astype(BF16)
    w_f = w_in[:, d:3 * d].astype(BF16)
    lb = jnp.cumsum(jax.nn.softmax(hgrn_lb_logits.astype(F32), axis=0), axis=0)[0].reshape(1, 2 * d)
    tn = 512
    nq = d // tn

    def qvg_epilogue(acc, j, aux, outs, cols):
        is_v = (j >= nq) & (j < 2 * nq)
        outs[0][:, cols] = jnp.where(is_v, acc, _silu(acc)).astype(BF16)

    (qvg,) = norm_matmul(
        "hgrn_qvg_proj", xf, norm_gains[0, 0], w_qvg, qvg_epilogue,
        [jax.ShapeDtypeStruct((t, 3 * d), BF16)], [pl.BlockSpec((1024, tn), lambda i, j: (i, j))], tn=tn)

    def f_epilogue(acc, j, aux, outs, cols):
        lbv = aux[0][:, cols]
        fgate = lbv + (1.0 - lbv) * jax.nn.sigmoid(acc)
        outs[0][:, cols] = jnp.log2(fgate)
        outs[1][:, cols] = (1.0 - fgate).astype(BF16)

    log2f, kk = norm_matmul(
        "hgrn_forget_proj", xf, norm_gains[0, 0], w_f, f_epilogue,
        [jax.ShapeDtypeStruct((t, 2 * d), F32), jax.ShapeDtypeStruct((t, 2 * d), BF16)],
        [pl.BlockSpec((1024, tn), lambda i, j: (i, j)), pl.BlockSpec((1024, tn), lambda i, j: (i, j))],
        aux=(lb,), aux_specs=(pl.BlockSpec((1, tn), lambda i, j: (0, j)),), tn=tn)

    o_f, o_b = gla_bidirectional(qvg, kk, log2f, batch=batch, seq=seq, heads=heads)

    tm = 512
    x1 = proj_residual(
        "hgrn_out_proj", (o_f, o_b, qvg, hgrn_onorm[0].reshape(1, d)),
        (pl.BlockSpec((tm, d), lambda i, j: (i, 0)), pl.BlockSpec((tm, d), lambda i, j: (i, 0)),
         pl.BlockSpec((tm, d), lambda i, j: (i, 2)), pl.BlockSpec((1, d), lambda i, j: (0, 0))),
        _hgrn_out_prologue, hgrn_w_out[0].astype(BF16), xf, tm=tm)

    fdim = ffn_w_gate.shape[2]
    fpad = -(-fdim // 512) * 512 - fdim
    wg = jnp.pad(ffn_w_gate[0], ((0, 0), (0, fpad))).astype(BF16)
    wu = jnp.pad(ffn_w_up[0], ((0, 0), (0, fpad))).astype(BF16)
    wd = jnp.pad(ffn_w_down[0], ((0, fpad), (0, 0))).astype(BF16)
    x2 = norm_swiglu_residual(x1, norm_gains[0, 1], wg, wu, wd)

    cos, sin = _rope_tables(seq, hd)
    qg = attn_q_gain[0].reshape(1, hd)
    kg = attn_k_gain[0].reshape(1, hd)
    tm_qkv = 1024
    pos_blocks = seq // tm_qkv

    def qkv_epilogue(acc, j, aux, outs, cols):
        cos_ref, sin_ref, qg_ref, kg_ref = aux
        o_ref = outs[0]

        def normed_rope(gain, scale):
            for h in range(acc.shape[1] // hd):
                y = _rms_rows(acc[:, h * hd:(h + 1) * hd], gain)
                y = y * cos_ref[...] + pltpu.roll(y, hd // 2, 1) * sin_ref[...]
                o_ref[:, cols.start + h * hd:cols.start + (h + 1) * hd] = y * scale

        @pl.when(j < nq)
        def _():
            normed_rope(qg_ref[...], hd ** -0.5)

        @pl.when((j >= nq) & (j < 2 * nq))
        def _():
            normed_rope(kg_ref[...], 1.0)

        @pl.when(j >= 2 * nq)
        def _():
            o_ref[:, cols] = acc

    (qkv,) = norm_matmul(
        "attn_qkv_proj", x2, norm_gains[1, 0], attn_w_qkv[0].astype(BF16), qkv_epilogue,
        [jax.ShapeDtypeStruct((t, 3 * d), F32)], [pl.BlockSpec((tm_qkv, tn), lambda i, j: (i, j))],
        aux=(cos, sin, qg, kg),
        aux_specs=(pl.BlockSpec((tm_qkv, hd), lambda i, j: (i % pos_blocks, 0)),
                   pl.BlockSpec((tm_qkv, hd), lambda i, j: (i % pos_blocks, 0)),
                   pl.BlockSpec((1, hd), lambda i, j: (0, 0)),
                   pl.BlockSpec((1, hd), lambda i, j: (0, 0))),
        tm=tm_qkv, tn=tn, sub=tn)

    attn = dilated_attention(qkv, batch=batch, seq=seq, heads=heads)
    x3 = proj_residual("attn_out_proj", (attn,), (pl.BlockSpec((tm, d), lambda i, j: (i, 0)),),
                       _copy_prologue, attn_w_out[0].astype(BF16), x2, tm=tm)

    tb = 512
    w_router = jnp.pad(moe_w_router[0].astype(F32), ((0, 0), (0, LANES - N_EXPERTS)))
    hn3, e_pad, p_pad = moe_router(x3, norm_gains[1, 1], w_router)
    slot_tok, blk_e, n_used, tok_slots = _routing_plan(e_pad[:, :TOP_K], tb)
    y = moe_experts(hn3, slot_tok, blk_e, n_used, moe_w_gate[0].astype(BF16),
                    moe_w_up[0].astype(BF16), moe_w_down[0].astype(BF16), tb)
    out = moe_combine(x3, y, tok_slots, p_pad)
    return out.reshape(batch, seq, d)
```

```python
import functools

import jax
import jax.numpy as jnp
from jax import lax
from jax.experimental import pallas as pl
from jax.experimental.pallas import tpu as pltpu

F32 = jnp.float32
BF16 = jnp.bfloat16
EPS = 1e-6
NEG_INF = -1e30
ROPE_THETA = 10000.0

HEAD_DIM = 128
GLA_CHUNK = 64
DIL_BRANCHES = ((128, 1), (512, 4), (2048, 16))
N_EXPERTS = 8
TOP_K = 2

LANES = 128
V7X_VMEM_BYTES = 64 * 1024 * 1024
VMEM_BUDGET = 56 * 1024 * 1024

_NT = (((1,), (1,)), ((), ()))
_TN = (((0,), (0,)), ((), ()))


def _params(semantics, vmem_bytes):
    return pltpu.CompilerParams(dimension_semantics=semantics,
                                vmem_limit_bytes=int(min(vmem_bytes, VMEM_BUDGET)))


def _silu(x):
    return x * jax.nn.sigmoid(x)


def _rms_rows(x, gain):
    ms = jnp.mean(x * x, axis=-1, keepdims=True)
    return x * lax.rsqrt(ms + EPS) * gain


def _norm_matmul_body(x_ref, g_ref, w_ref, *rest, n_aux, epilogue, sub):
    aux, outs, hn_ref = rest[:n_aux], rest[n_aux:-1], rest[-1]
    j = pl.program_id(1)

    @pl.when(j == 0)
    def _():
        hn_ref[...] = _rms_rows(x_ref[...], g_ref[...]).astype(BF16)

    for s in range(w_ref.shape[1] // sub):
        cols = slice(s * sub, (s + 1) * sub)
        acc = jnp.dot(hn_ref[...], w_ref[:, cols], preferred_element_type=F32)
        epilogue(acc, j, aux, outs, cols)


def norm_matmul(name, x, gain, w, epilogue, out_shapes, out_specs, aux=(), aux_specs=(), tm=1024, tn=512,
                sub=256):
    m, d = x.shape
    n = w.shape[1]
    assert m % tm == 0 and n % tn == 0
    out_bytes = sum(2 * tm * tn * jnp.dtype(s.dtype).itemsize for s in out_shapes)
    vmem = 2 * tm * d * 4 + tm * d * 2 + 2 * d * tn * 2 + out_bytes + 4 * tm * tn * 4 + (4 << 20)
    return pl.pallas_call(
        functools.partial(_norm_matmul_body, n_aux=len(aux), epilogue=epilogue, sub=min(sub, tn)),
        grid=(m // tm, n // tn),
        in_specs=[pl.BlockSpec((tm, d), lambda i, j: (i, 0)),
                  pl.BlockSpec((1, d), lambda i, j: (0, 0)),
                  pl.BlockSpec((d, tn), lambda i, j: (0, j)),
                  *aux_specs],
        out_specs=out_specs,
        out_shape=out_shapes,
        scratch_shapes=[pltpu.VMEM((tm, d), BF16)],
        compiler_params=_params(("parallel", "arbitrary"), vmem),
        name=name,
    )(x, gain.reshape(1, d), w, *aux)


def _gla_chunks(chains):
    c = chains[0]["q"].shape[0]
    n_levels = c.bit_length()
    for ch in chains:
        hi = ch["lf2"].astype(BF16)
        lo = (ch["lf2"] - hi.astype(F32)).astype(BF16)
        ch["s"] = jnp.dot(ch["sums"], jnp.concatenate([hi, lo], axis=0), preferred_element_type=F32)
    for ch in chains:
        qd = ch["q"] * jnp.exp2(ch["s"][:c]).astype(BF16)
        ch["o"] = lax.dot_general(qd, ch["st"].astype(BF16), _NT, preferred_element_type=F32)
        ch["a"] = jnp.zeros((c, 2 * c), F32)
    for level in range(0, n_levels, 2):
        for ch in chains:
            qs, ks = [], []
            for l in (level, level + 1):
                if l == 0:
                    qs.append(ch["q"])
                    ks.append(ch["k"])
                elif l < n_levels:
                    e = jnp.exp2(-jnp.abs(ch["s"][l * c:(l + 1) * c])).astype(BF16)
                    qs.append(ch["q"] * e)
                    ks.append(ch["k"] * e)
            if len(ks) == 1:
                ks = ks * 2
            p = lax.dot_general(jnp.concatenate(qs, axis=0), jnp.concatenate(ks, axis=0), _NT,
                                preferred_element_type=F32)
            for i in range(len(qs)):
                ch["a"] = jnp.where(ch["lvl"] == 2 * (level + i) + i, p[i * c:(i + 1) * c, :], ch["a"])
    outs = []
    for ch in chains:
        g = ch["s"][:c]
        g_tot = g[c - 1:c, :] if ch["fwd"] else g[0:1, :]
        vv = jnp.concatenate([ch["v"], ch["v"]], axis=0)
        o = ch["o"] + jnp.dot(ch["a"].astype(BF16), vv, preferred_element_type=F32)
        kd = ch["k"] * jnp.exp2(g_tot - g).astype(BF16)
        st_new = (ch["st"] * jnp.exp2(g_tot)
                  + lax.dot_general(ch["v"], kd, _TN, preferred_element_type=F32))
        outs.append((o, st_new))
    return outs


def _gla_tables(c, fwd):
    ti = lax.broadcasted_iota(jnp.int32, (c, c), 0)
    ui = lax.broadcasted_iota(jnp.int32, (c, c), 1)

    def cum(row):
        return ((ui <= row) if fwd else (ui >= row)).astype(jnp.int32)

    blocks = [cum(ti)]
    half = 1
    while half < c:
        boundary = (ti & -(2 * half)) + (half - 1 if fwd else half)
        blocks.append(cum(ti) - cum(boundary))
        half *= 2
    x = ti ^ ui
    top_bit = sum((x >= (1 << b)).astype(jnp.int32) for b in range(1, c.bit_length() - 1))
    lvl = jnp.where(ti == ui, 0, jnp.where((ui < ti) if fwd else (ui > ti), 1 + top_bit, -1))
    sums = jnp.concatenate(blocks, axis=0).astype(F32).astype(BF16)
    codes = jnp.concatenate([2 * lvl, 2 * lvl + 1], axis=1)
    return jnp.concatenate([sums, sums], axis=1), codes


def _gla_body(qf_ref, kf_ref, vf_ref, lf_ref, qb_ref, kb_ref, vb_ref, lb_ref, of_ref, ob_ref,
              st_ref, sums_ref, lvl_ref, *, chunk):
    rows, width = qf_ref.shape
    nch = rows // chunk
    c = chunk

    @pl.when(pl.program_id(2) == 0)
    def _():
        st_ref[...] = jnp.zeros_like(st_ref)

    dirs = []
    for d, (fwd, refs) in enumerate(((True, (qf_ref, kf_ref, vf_ref, lf_ref, of_ref)),
                                     (False, (qb_ref, kb_ref, vb_ref, lb_ref, ob_ref)))):
        sums, lvl = _gla_tables(c, fwd)
        sums_ref[d] = sums
        lvl_ref[d] = lvl
        dirs.append((fwd, refs))

    def one_chunk(ci, carry):
        chains, dests = [], []
        for d, (fwd, (q_ref, k_ref, v_ref, l_ref, o_ref)) in enumerate(dirs):
            r0 = pl.multiple_of((ci if fwd else nch - 1 - ci) * c, c)
            for h in range(width // HEAD_DIM):
                hs = slice(h * HEAD_DIM, (h + 1) * HEAD_DIM)
                chains.append(dict(q=q_ref[pl.ds(r0, c), hs], k=k_ref[pl.ds(r0, c), hs],
                                   v=v_ref[pl.ds(r0, c), hs], lf2=l_ref[pl.ds(r0, c), hs],
                                   st=st_ref[d, h], lvl=lvl_ref[d], fwd=fwd, sums=sums_ref[d]))
                dests.append((o_ref, r0, hs, d, h))
        for (o, st_new), (o_ref, r0, hs, d, h) in zip(_gla_chunks(chains), dests):
            o_ref[pl.ds(r0, c), hs] = o.astype(o_ref.dtype)
            st_ref[d, h] = st_new
        return carry

    lax.fori_loop(0, nch, one_chunk, 0)


def gla_bidirectional(qvg, kk, log2f, *, batch, seq, heads, heads_per_step=4, rows_per_step=512):
    t = batch * seq
    hg = heads_per_step
    width = hg * HEAD_DIM
    rb = rows_per_step
    ns = seq // rb
    ng = heads // hg

    def spec(fwd, seg):
        return pl.BlockSpec((rb, width),
                            lambda b, g, s: (b * ns + (s if fwd else ns - 1 - s), seg * ng + g))

    return pl.pallas_call(
        functools.partial(_gla_body, chunk=GLA_CHUNK),
        grid=(batch, ng, ns),
        in_specs=[spec(True, 0), spec(True, 0), spec(True, 1), spec(True, 0),
                  spec(False, 0), spec(False, 1), spec(False, 1), spec(False, 1)],
        out_specs=[spec(True, 0), spec(False, 0)],
        out_shape=[jax.ShapeDtypeStruct((t, heads * HEAD_DIM), BF16)] * 2,
        scratch_shapes=[pltpu.VMEM((2, hg, HEAD_DIM, HEAD_DIM), F32),
                        pltpu.VMEM((2, GLA_CHUNK * GLA_CHUNK.bit_length(), 2 * GLA_CHUNK), BF16),
                        pltpu.VMEM((2, GLA_CHUNK, 2 * GLA_CHUNK), jnp.int32)],
        compiler_params=_params(("parallel", "parallel", "arbitrary"), 32 << 20),
        name="gla_bidir",
    )(qvg, kk, qvg, log2f, qvg, kk, qvg, log2f)


def _proj_residual_body(*refs, n_pro, prologue):
    pro, (w_ref, x_ref, o_ref, y_ref) = refs[:n_pro], refs[n_pro:]

    @pl.when(pl.program_id(1) == 0)
    def _():
        prologue(pro, y_ref)

    o_ref[...] = x_ref[...] + jnp.dot(y_ref[...], w_ref[...], preferred_element_type=F32)


def proj_residual(name, pro_inputs, pro_specs, prologue, w, xres, tm=512, tn=512):
    m, n = xres.shape
    kdim = w.shape[0]
    pro_bytes = sum(2 * tm * kdim * jnp.dtype(a.dtype).itemsize for a in pro_inputs)
    vmem = pro_bytes + tm * kdim * 2 + 2 * kdim * tn * 2 + 4 * tm * tn * 4 + 4 * tm * kdim * 4 + (4 << 20)
    return pl.pallas_call(
        functools.partial(_proj_residual_body, n_pro=len(pro_inputs), prologue=prologue),
        grid=(m // tm, n // tn),
        in_specs=[*pro_specs,
                  pl.BlockSpec((kdim, tn), lambda i, j: (0, j)),
                  pl.BlockSpec((tm, tn), lambda i, j: (i, j))],
        out_specs=pl.BlockSpec((tm, tn), lambda i, j: (i, j)),
        out_shape=jax.ShapeDtypeStruct((m, n), F32),
        scratch_shapes=[pltpu.VMEM((tm, kdim), BF16)],
        compiler_params=_params(("parallel", "arbitrary"), vmem),
        name=name,
    )(*pro_inputs, w, xres)


def _hgrn_out_prologue(pro, y_ref):
    of_ref, ob_ref, gate_ref, gain_ref = pro
    for h in range(of_ref.shape[1] // HEAD_DIM):
        hs = slice(h * HEAD_DIM, (h + 1) * HEAD_DIM)
        o = of_ref[:, hs].astype(F32) + ob_ref[:, hs].astype(F32)
        y = _rms_rows(o, gain_ref[:, hs]) * gate_ref[:, hs].astype(F32)
        y_ref[:, hs] = y.astype(BF16)


def _copy_prologue(pro, y_ref):
    y_ref[...] = pro[0][...]


def _swiglu_body(x_ref, g_ref, wg_ref, wu_ref, wd_ref, o_ref, hn_ref):
    @pl.when(pl.program_id(1) == 0)
    def _():
        x = x_ref[...]
        hn_ref[...] = _rms_rows(x, g_ref[...]).astype(BF16)
        o_ref[...] = x

    hn = hn_ref[...]
    a = jnp.dot(hn, wg_ref[...], preferred_element_type=F32)
    u = jnp.dot(hn, wu_ref[...], preferred_element_type=F32)
    h = (_silu(a) * u).astype(BF16)
    o_ref[...] += jnp.dot(h, wd_ref[...], preferred_element_type=F32)


def norm_swiglu_residual(x, gain, wg, wu, wd, tm=512, tf=512):
    m, d = x.shape
    f = wg.shape[1]
    assert m % tm == 0 and f % tf == 0
    vmem = 4 * tm * d * 4 + tm * d * 2 + 2 * (2 * d * tf + tf * d) * 2 + 3 * tm * tf * 4 + 2 * tm * d * 4 + (4 << 20)
    return pl.pallas_call(
        _swiglu_body,
        grid=(m // tm, f // tf),
        in_specs=[pl.BlockSpec((tm, d), lambda i, j: (i, 0)),
                  pl.BlockSpec((1, d), lambda i, j: (0, 0)),
                  pl.BlockSpec((d, tf), lambda i, j: (0, j)),
                  pl.BlockSpec((d, tf), lambda i, j: (0, j)),
                  pl.BlockSpec((tf, d), lambda i, j: (j, 0))],
        out_specs=pl.BlockSpec((tm, d), lambda i, j: (i, 0)),
        out_shape=jax.ShapeDtypeStruct((m, d), F32),
        scratch_shapes=[pltpu.VMEM((tm, d), BF16)],
        compiler_params=_params(("parallel", "arbitrary"), vmem),
        name="dense_swiglu",
    )(x, gain.reshape(1, d), wg, wu, wd)


def _dilated_attn_body(*refs, hg, branches, tq):
    q_refs, k_refs, v_refs = (refs[i * hg:(i + 1) * hg] for i in range(3))
    o_ref = refs[3 * hg]
    acc_refs, m_refs, l_refs = (refs[3 * hg + 1 + i * hg:3 * hg + 1 + (i + 1) * hg] for i in range(3))
    bias_ref = refs[6 * hg + 1]
    seq = q_refs[0].shape[0]
    steps = branches[0][1]
    tq_max = min(tq, seq)
    wk_max = min(seq, tq_max + 2 * steps)

    d = (lax.broadcasted_iota(jnp.int32, (tq_max, wk_max), 1)
         - lax.broadcasted_iota(jnp.int32, (tq_max, wk_max), 0))
    for shift in range(3):
        bias_ref[shift] = jnp.where(jnp.abs(d - shift * steps) <= steps, 0.0, NEG_INF)

    for bi, (r, br_steps) in enumerate(branches):
        assert br_steps == steps
        n = seq // r
        tqb = min(tq, n)
        wk = min(n, tqb + 2 * steps)
        nqb = n // tqb

        def block(idx, carry, bi=bi, r=r, n=n, tqb=tqb, wk=wk, nqb=nqb):
            rho = idx // nqb
            q0 = (idx - rho * nqb) * tqb
            k0 = jnp.clip(q0 - steps, 0, n - wk)
            bias = bias_ref[(q0 - k0) // steps, :tqb, :wk]

            def rows(c0, cnt):
                if r == 1:
                    return pl.ds(pl.multiple_of(c0, steps), cnt)
                return pl.ds(rho + r * c0, cnt, stride=r)

            s = [lax.dot_general(q_refs[h][rows(q0, tqb), :].astype(BF16),
                                 k_refs[h][rows(k0, wk), :].astype(BF16), _NT,
                                 preferred_element_type=F32) + bias for h in range(hg)]
            m_new = [jnp.broadcast_to(jnp.max(s[h], axis=-1, keepdims=True), (tqb, HEAD_DIM))
                     for h in range(hg)]
            p = [jnp.exp(s[h] - m_new[h][:, :1]) for h in range(hg)]
            l_new = [jnp.broadcast_to(jnp.sum(p[h], axis=-1, keepdims=True), (tqb, HEAD_DIM))
                     for h in range(hg)]
            acc_new = [jnp.dot(p[h].astype(BF16), v_refs[h][rows(k0, wk), :].astype(BF16),
                               preferred_element_type=F32) for h in range(hg)]
            for h in range(hg):
                if bi > 0:
                    m_old = m_refs[h][rows(q0, tqb), :]
                    m_all = jnp.maximum(m_old, m_new[h])
                    w_old = jnp.exp(m_old - m_all)
                    w_new = jnp.exp(m_new[h] - m_all)
                    acc_new[h] = acc_refs[h][rows(q0, tqb), :] * w_old + acc_new[h] * w_new
                    l_new[h] = l_refs[h][rows(q0, tqb), :] * w_old + l_new[h] * w_new
                    m_new[h] = m_all
                acc_refs[h][rows(q0, tqb), :] = acc_new[h]
                l_refs[h][rows(q0, tqb), :] = l_new[h]
                m_refs[h][rows(q0, tqb), :] = m_new[h]
            return carry

        lax.fori_loop(0, r * nqb, block, 0)

    for h in range(hg):
        o_ref[:, h * HEAD_DIM:(h + 1) * HEAD_DIM] = (acc_refs[h][...] / l_refs[h][...]).astype(o_ref.dtype)


def dilated_attention(qkv, *, batch, seq, heads, heads_per_step=2, tq=256):
    t = batch * seq
    hd = HEAD_DIM
    hg = heads_per_step
    branches = tuple((dil, window // (2 * dil)) for window, dil in DIL_BRANCHES)
    steps = branches[0][1]
    tq_max = min(tq, seq)
    specs = lambda part: [pl.BlockSpec((seq, hd), lambda b, g, h=h: (b, part * heads + g * hg + h))
                          for h in range(hg)]
    return pl.pallas_call(
        functools.partial(_dilated_attn_body, hg=hg, branches=branches, tq=tq),
        grid=(batch, heads // hg),
        in_specs=specs(0) + specs(1) + specs(2),
        out_specs=pl.BlockSpec((seq, hg * hd), lambda b, g: (b, g)),
        out_shape=jax.ShapeDtypeStruct((t, heads * hd), BF16),
        scratch_shapes=([pltpu.VMEM((seq, hd), F32)] * (3 * hg)
                        + [pltpu.VMEM((3, tq_max, min(seq, tq_max + 2 * steps)), F32)]),
        compiler_params=_params(("parallel", "parallel"),
                                hg * seq * hd * (2 * 3 * 4 + 2 * 2 + 3 * 4) + (16 << 20)),
        name="dilated_attn",
    )(*([qkv] * (3 * hg)))


def _router_body(x_ref, g_ref, wr_ref, hn_ref, e_ref, p_ref):
    hn = _rms_rows(x_ref[...], g_ref[...])
    hn_ref[...] = hn
    logits = jnp.dot(hn, wr_ref[...], precision=lax.Precision.HIGHEST, preferred_element_type=F32)
    lane = lax.broadcasted_iota(jnp.int32, logits.shape, 1)
    logits = jnp.where(lane < N_EXPERTS, logits, -jnp.inf)
    m1 = jnp.max(logits, axis=-1, keepdims=True)
    i1 = jnp.min(jnp.where(logits == m1, lane, LANES), axis=-1, keepdims=True)
    rest = jnp.where(lane == i1, -jnp.inf, logits)
    m2 = jnp.max(rest, axis=-1, keepdims=True)
    i2 = jnp.min(jnp.where(rest == m2, lane, LANES), axis=-1, keepdims=True)
    e2 = jnp.exp(m2 - m1)
    den = 1.0 + e2
    e_ref[...] = jnp.where(lane == 0, i1, jnp.where(lane == 1, i2, 0))
    p_ref[...] = jnp.where(lane == 0, 1.0 / den, jnp.where(lane == 1, e2 / den, 0.0))


def moe_router(x, gain, w_router_padded, tm=512):
    m, d = x.shape
    return pl.pallas_call(
        _router_body,
        grid=(m // tm,),
        in_specs=[pl.BlockSpec((tm, d), lambda i: (i, 0)),
                  pl.BlockSpec((1, d), lambda i: (0, 0)),
                  pl.BlockSpec((d, LANES), lambda i: (0, 0))],
        out_specs=[pl.BlockSpec((tm, d), lambda i: (i, 0)),
                   pl.BlockSpec((tm, LANES), lambda i: (i, 0)),
                   pl.BlockSpec((tm, LANES), lambda i: (i, 0))],
        out_shape=[jax.ShapeDtypeStruct((m, d), F32),
                   jax.ShapeDtypeStruct((m, LANES), jnp.int32),
                   jax.ShapeDtypeStruct((m, LANES), F32)],
        compiler_params=_params(("parallel",), 6 * tm * d * 4 + (8 << 20)),
        name="moe_router",
    )(x, gain.reshape(1, d), w_router_padded)


def _gather_rows_per_step(tb, nf):
    per_step = -(-tb // nf)
    while (per_step * nf) % 8:
        per_step += 1
    return per_step


def _expert_body(blk_e_ref, n_used_ref, tok_ref, tok_next_ref, hn_hbm, wg_ref, wu_ref, wd_ref,
                 y_ref, xf_ref, xb_ref, sem, *, nf, nblk):
    b = pl.program_id(0)
    f = pl.program_id(1)
    tb = y_ref.shape[0]
    n_used = n_used_ref[0]
    used = b < n_used
    slot = lax.rem(b, 2)
    per_step = _gather_rows_per_step(tb, nf)

    def row_copy(toks, j, dst_slot):
        return pltpu.make_async_copy(hn_hbm.at[pl.ds(toks[0, jnp.minimum(j, tb - 1)], 1), :],
                                     xf_ref.at[dst_slot, pl.ds(j, 1), :], sem.at[dst_slot])

    def wait_gather(dst_slot):
        rows = per_step * nf
        pltpu.make_async_copy(hn_hbm.at[pl.ds(0, rows), :], xf_ref.at[dst_slot], sem.at[dst_slot]).wait()

    @pl.when((b == 0) & (f == 0))
    def _():
        def start(j, c):
            row_copy(tok_ref, j, 0).start()
            return c
        lax.fori_loop(0, per_step * nf, start, 0)

    @pl.when(used & (f == 0))
    def _():
        wait_gather(slot)
        xb_ref[...] = xf_ref[slot, :tb, :].astype(BF16)
        y_ref[...] = jnp.zeros_like(y_ref)

    @pl.when(~used & (f == 0))
    def _():
        y_ref[...] = jnp.zeros_like(y_ref)

    @pl.when(used)
    def _():
        for j in range(per_step):
            row_copy(tok_next_ref, f * per_step + j, 1 - slot).start()
        xb = xb_ref[...]
        a = jnp.dot(xb, wg_ref[...], preferred_element_type=F32)
        u = jnp.dot(xb, wu_ref[...], preferred_element_type=F32)
        h = (_silu(a) * u).astype(BF16)
        y_ref[...] += jnp.dot(h, wd_ref[...], preferred_element_type=F32)

    @pl.when(used & (f == nf - 1) & (b + 1 >= n_used))
    def _():
        wait_gather(1 - slot)


def moe_experts(hn, slot_tok, blk_e, n_used, wg, wu, wd, tb, tf=1024):
    t, d = hn.shape
    p = slot_tok.shape[0]
    nblk = p // tb
    fdim = wg.shape[2]
    nf = fdim // tf

    def live(b, n_used_ref):
        return jnp.minimum(b, n_used_ref[0] - 1)

    def wmap_cols(b, f, blk_e_ref, n_used_ref):
        return (blk_e_ref[live(b, n_used_ref)], 0, jnp.where(b < n_used_ref[0], f, nf - 1))

    def wmap_rows(b, f, blk_e_ref, n_used_ref):
        return (blk_e_ref[live(b, n_used_ref)], jnp.where(b < n_used_ref[0], f, nf - 1), 0)

    vmem = (tb * d * (2 * 4 + 2) + tb * d * 4 + 2 * (2 * d * tf + tf * d) * 2 + 3 * tb * tf * 4
            + tb * d * 4 + (4 << 20))
    toks = slot_tok.reshape(nblk, 1, tb)
    grid_spec = pltpu.PrefetchScalarGridSpec(
        num_scalar_prefetch=2,
        grid=(nblk, nf),
        in_specs=[pl.BlockSpec((None, 1, tb), lambda b, f, *_: (b, 0, 0), memory_space=pltpu.SMEM),
                  pl.BlockSpec((None, 1, tb), lambda b, f, *_: (jnp.minimum(b + 1, nblk - 1), 0, 0),
                               memory_space=pltpu.SMEM),
                  pl.BlockSpec(memory_space=pl.ANY),
                  pl.BlockSpec((None, d, tf), wmap_cols),
                  pl.BlockSpec((None, d, tf), wmap_cols),
                  pl.BlockSpec((None, tf, d), wmap_rows)],
        out_specs=pl.BlockSpec((tb, d), lambda b, f, *_: (b, 0), pipeline_mode=pl.Buffered(1)),
        scratch_shapes=[pltpu.VMEM((2, _gather_rows_per_step(tb, nf) * nf, d), F32), pltpu.VMEM((tb, d), BF16),
                        pltpu.SemaphoreType.DMA((2,))],
    )
    return pl.pallas_call(
        functools.partial(_expert_body, nf=nf, nblk=nblk),
        grid_spec=grid_spec,
        out_shape=jax.ShapeDtypeStruct((p, d), F32),
        compiler_params=_params(("arbitrary", "arbitrary"), vmem),
        name="moe_experts",
    )(blk_e, n_used, toks, toks, hn, wg, wu, wd)


def _combine_body(slot_ref, slot_next_ref, x_ref, p_ref, y_hbm, o_ref, buf_ref, sem):
    i = pl.program_id(0)
    nt = pl.num_programs(0)
    tm = x_ref.shape[0]
    cur = lax.rem(i, 2)

    def row_copy(slots, j, half):
        k, r = divmod(j, tm)
        return pltpu.make_async_copy(y_hbm.at[pl.ds(slots[0, j], 1), :],
                                     buf_ref.at[half, k, pl.ds(r, 1), :], sem.at[half])

    def wait_tile(half):
        for k in range(TOP_K):
            pltpu.make_async_copy(y_hbm.at[pl.ds(0, tm), :], buf_ref.at[half, k], sem.at[half]).wait()

    @pl.when(i == 0)
    def _():
        for j in range(TOP_K * tm):
            row_copy(slot_ref, j, 0).start()

    for j in range(TOP_K * tm):
        row_copy(slot_next_ref, j, 1 - cur).start()
    wait_tile(cur)
    gates = p_ref[...]
    o_ref[...] = x_ref[...] + (gates[:, 0:1] * buf_ref[cur, 0] + gates[:, 1:2] * buf_ref[cur, 1])

    @pl.when(i == nt - 1)
    def _():
        wait_tile(1 - cur)


def moe_combine(x, y, tok_slots, gates, tm=256):
    t, d = x.shape
    nt = t // tm
    slots = tok_slots.reshape(nt, tm, TOP_K).transpose(0, 2, 1).reshape(nt, 1, TOP_K * tm)
    slot_spec = lambda index_map: pl.BlockSpec((None, 1, TOP_K * tm), index_map, memory_space=pltpu.SMEM)
    return pl.pallas_call(
        _combine_body,
        grid=(nt,),
        in_specs=[slot_spec(lambda i: (i, 0, 0)),
                  slot_spec(lambda i: (jnp.minimum(i + 1, nt - 1), 0, 0)),
                  pl.BlockSpec((tm, d), lambda i: (i, 0)),
                  pl.BlockSpec((tm, LANES), lambda i: (i, 0)),
                  pl.BlockSpec(memory_space=pl.ANY)],
        out_specs=pl.BlockSpec((tm, d), lambda i: (i, 0)),
        out_shape=jax.ShapeDtypeStruct((t, d), F32),
        scratch_shapes=[pltpu.VMEM((2, TOP_K, tm, d), F32), pltpu.SemaphoreType.DMA((2,))],
        compiler_params=_params(("arbitrary",), (4 + 2 * TOP_K) * tm * d * 4 + (4 << 20)),
        name="moe_combine",
    )(slots, slots, x, gates, y)


def _routing_plan(top_e, tb):
    t = top_e.shape[0]
    n = t * TOP_K
    e_flat = top_e.reshape(n)
    onehot = (e_flat[:, None] == jnp.arange(N_EXPERTS, dtype=jnp.int32)[None, :]).astype(jnp.int32)
    rank = jnp.take_along_axis(jnp.cumsum(onehot, axis=0) - onehot, e_flat[:, None], axis=1)[:, 0]
    counts = jnp.sum(onehot, axis=0)
    padded = (counts + tb - 1) // tb * tb
    pend = jnp.cumsum(padded)
    dest = (pend - padded)[e_flat] + rank
    nblk = -(-n // tb) + N_EXPERTS
    p = nblk * tb
    tok_flat = jnp.arange(n, dtype=jnp.int32) // TOP_K
    slot_tok = jnp.zeros((p,), jnp.int32).at[dest].set(tok_flat)
    blk_start = jnp.arange(nblk, dtype=pend.dtype) * tb
    blk_e = jnp.minimum(jnp.searchsorted(pend, blk_start, side='right'), N_EXPERTS - 1).astype(jnp.int32)
    n_used = (pend[-1] // tb).astype(jnp.int32).reshape(1)
    return slot_tok, blk_e, n_used, dest.reshape(t, TOP_K).astype(jnp.int32)


def _rope_tables(seq, hd):
    half = hd // 2
    inv_freq = ROPE_THETA ** (-jnp.arange(half, dtype=F32) * 2.0 / hd)
    ang = jnp.arange(seq, dtype=F32)[:, None] * inv_freq[None, :]
    cos, sin = jnp.cos(ang), jnp.sin(ang)
    return jnp.concatenate([cos, cos], axis=-1), jnp.concatenate([-sin, sin], axis=-1)


def kernel(x, norm_gains, hgrn_w_in, hgrn_lb_logits, hgrn_onorm, hgrn_w_out, attn_w_qkv, attn_q_gain,
           attn_k_gain, attn_w_out, ffn_w_gate, ffn_w_up, ffn_w_down, moe_w_router, moe_w_gate, moe_w_up,
           moe_w_down):
    batch, seq, d = x.shape
    t = batch * seq
    heads = d // HEAD_DIM
    hd = HEAD_DIM
    xf = x.reshape(t, d)

    w_in = hgrn_w_in[0]
    w_qvg = jnp.concatenate([w_in[:, :d], w_in[:, 3 * d:5 * d]], axis=1).astype(BF16)
    w_f = w_in[:, d:3 * d].astype(BF16)
    lb = jnp.cumsum(jax.nn.softmax(hgrn_lb_logits.astype(F32), axis=0), axis=0)[0].reshape(1, 2 * d)
    tn = min(1024, d)
    nq = d // tn

    def qvg_epilogue(acc, j, aux, outs, cols):
        is_v = (j >= nq) & (j < 2 * nq)
        outs[0][:, cols] = jnp.where(is_v, acc, _silu(acc)).astype(BF16)

    (qvg,) = norm_matmul(
        "hgrn_qvg_proj", xf, norm_gains[0, 0], w_qvg, qvg_epilogue,
        [jax.ShapeDtypeStruct((t, 3 * d), BF16)], [pl.BlockSpec((1024, tn), lambda i, j: (i, j))], tn=tn)

    def f_epilogue(acc, j, aux, outs, cols):
        lbv = aux[0][:, cols]
        fgate = lbv + (1.0 - lbv) * jax.nn.sigmoid(acc)
        outs[0][:, cols] = jnp.log2(fgate)
        outs[1][:, cols] = (1.0 - fgate).astype(BF16)

    log2f, kk = norm_matmul(
        "hgrn_forget_proj", xf, norm_gains[0, 0], w_f, f_epilogue,
        [jax.ShapeDtypeStruct((t, 2 * d), F32), jax.ShapeDtypeStruct((t, 2 * d), BF16)],
        [pl.BlockSpec((1024, tn), lambda i, j: (i, j)), pl.BlockSpec((1024, tn), lambda i, j: (i, j))],
        aux=(lb,), aux_specs=(pl.BlockSpec((1, tn), lambda i, j: (0, j)),), tn=tn)

    o_f, o_b = gla_bidirectional(qvg, kk, log2f, batch=batch, seq=seq, heads=heads)

    tm = 512
    x1 = proj_residual(
        "hgrn_out_proj", (o_f, o_b, qvg, hgrn_onorm[0].reshape(1, d)),
        (pl.BlockSpec((tm, d), lambda i, j: (i, 0)), pl.BlockSpec((tm, d), lambda i, j: (i, 0)),
         pl.BlockSpec((tm, d), lambda i, j: (i, 2)), pl.BlockSpec((1, d), lambda i, j: (0, 0))),
        _hgrn_out_prologue, hgrn_w_out[0].astype(BF16), xf, tm=tm)

    fdim = ffn_w_gate.shape[2]
    fpad = -(-fdim // 512) * 512 - fdim
    wg = jnp.pad(ffn_w_gate[0], ((0, 0), (0, fpad))).astype(BF16)
    wu = jnp.pad(ffn_w_up[0], ((0, 0), (0, fpad))).astype(BF16)
    wd = jnp.pad(ffn_w_down[0], ((0, fpad), (0, 0))).astype(BF16)
    x2 = norm_swiglu_residual(x1, norm_gains[0, 1], wg, wu, wd)

    cos, sin = _rope_tables(seq, hd)
    qg = attn_q_gain[0].reshape(1, hd)
    kg = attn_k_gain[0].reshape(1, hd)
    tm_qkv = 1024
    tn_qkv = 512
    nq_qkv = d // tn_qkv
    pos_blocks = seq // tm_qkv

    def qkv_epilogue(acc, j, aux, outs, cols):
        cos_ref, sin_ref, qg_ref, kg_ref = aux
        o_ref = outs[0]

        def normed_rope(gain, scale):
            for h in range(acc.shape[1] // hd):
                y = _rms_rows(acc[:, h * hd:(h + 1) * hd], gain)
                y = y * cos_ref[...] + pltpu.roll(y, hd // 2, 1) * sin_ref[...]
                o_ref[:, cols.start + h * hd:cols.start + (h + 1) * hd] = y * scale

        @pl.when(j < nq_qkv)
        def _():
            normed_rope(qg_ref[...], hd ** -0.5)

        @pl.when((j >= nq_qkv) & (j < 2 * nq_qkv))
        def _():
            normed_rope(kg_ref[...], 1.0)

        @pl.when(j >= 2 * nq_qkv)
        def _():
            o_ref[:, cols] = acc

    (qkv,) = norm_matmul(
        "attn_qkv_proj", x2, norm_gains[1, 0], attn_w_qkv[0].astype(BF16), qkv_epilogue,
        [jax.ShapeDtypeStruct((t, 3 * d), F32)], [pl.BlockSpec((tm_qkv, tn_qkv), lambda i, j: (i, j))],
        aux=(cos, sin, qg, kg),
        aux_specs=(pl.BlockSpec((tm_qkv, hd), lambda i, j: (i % pos_blocks, 0)),
                   pl.BlockSpec((tm_qkv, hd), lambda i, j: (i % pos_blocks, 0)),
                   pl.BlockSpec((1, hd), lambda i, j: (0, 0)),
                   pl.BlockSpec((1, hd), lambda i, j: (0, 0))),
        tm=tm_qkv, tn=tn_qkv, sub=tn_qkv)

    attn = dilated_attention(qkv, batch=batch, seq=seq, heads=heads)
    x3 = proj_residual("attn_out_proj", (attn,), (pl.BlockSpec((tm, d), lambda i, j: (i, 0)),),
                       _copy_prologue, attn_w_out[0].astype(BF16), x2, tm=tm)

    tb = 512
    w_router = jnp.pad(moe_w_router[0].astype(F32), ((0, 0), (0, LANES - N_EXPERTS)))
    hn3, e_pad, p_pad = moe_router(x3, norm_gains[1, 1], w_router)
    slot_tok, blk_e, n_used, tok_slots = _routing_plan(e_pad[:, :TOP_K], tb)
    y = moe_experts(hn3, slot_tok, blk_e, n_used, moe_w_gate[0].astype(BF16),
                    moe_w_up[0].astype(BF16), moe_w_down[0].astype(BF16), tb)
    out = moe_combine(x3, y, tok_slots, p_pad)
    return out.reshape(batch, seq, d)
```

```python
import functools

import jax
import jax.numpy as jnp
from jax import lax
from jax.experimental import pallas as pl
from jax.experimental.pallas import tpu as pltpu

F32 = jnp.float32
BF16 = jnp.bfloat16
EPS = 1e-6
NEG_INF = -1e30
ROPE_THETA = 10000.0

HEAD_DIM = 128
GLA_CHUNK = 64
DIL_BRANCHES = ((128, 1), (512, 4), (2048, 16))
N_EXPERTS = 8
TOP_K = 2

LANES = 128
V7X_VMEM_BYTES = 64 * 1024 * 1024
VMEM_BUDGET = 56 * 1024 * 1024

_NT = (((1,), (1,)), ((), ()))
_TN = (((0,), (0,)), ((), ()))


def _params(semantics, vmem_bytes):
    return pltpu.CompilerParams(dimension_semantics=semantics,
                                vmem_limit_bytes=int(min(vmem_bytes, VMEM_BUDGET)))


def _silu(x):
    return x * jax.nn.sigmoid(x)


def _rms_rows(x, gain):
    ms = jnp.mean(x * x, axis=-1, keepdims=True)
    return x * lax.rsqrt(ms + EPS) * gain


def _norm_matmul_body(x_ref, g_ref, w_ref, *rest, n_aux, epilogue, sub):
    aux, outs, hn_ref = rest[:n_aux], rest[n_aux:-1], rest[-1]
    j = pl.program_id(1)

    @pl.when(j == 0)
    def _():
        hn_ref[...] = _rms_rows(x_ref[...], g_ref[...]).astype(BF16)

    for s in range(w_ref.shape[1] // sub):
        cols = slice(s * sub, (s + 1) * sub)
        acc = jnp.dot(hn_ref[...], w_ref[:, cols], preferred_element_type=F32)
        epilogue(acc, j, aux, outs, cols)


def norm_matmul(name, x, gain, w, epilogue, out_shapes, out_specs, aux=(), aux_specs=(), tm=1024, tn=512,
                sub=256):
    m, d = x.shape
    n = w.shape[1]
    assert m % tm == 0 and n % tn == 0
    out_bytes = sum(2 * tm * tn * jnp.dtype(s.dtype).itemsize for s in out_shapes)
    vmem = 2 * tm * d * 4 + tm * d * 2 + 2 * d * tn * 2 + out_bytes + 4 * tm * tn * 4 + (4 << 20)
    return pl.pallas_call(
        functools.partial(_norm_matmul_body, n_aux=len(aux), epilogue=epilogue, sub=min(sub, tn)),
        grid=(m // tm, n // tn),
        in_specs=[pl.BlockSpec((tm, d), lambda i, j: (i, 0)),
                  pl.BlockSpec((1, d), lambda i, j: (0, 0)),
                  pl.BlockSpec((d, tn), lambda i, j: (0, j)),
                  *aux_specs],
        out_specs=out_specs,
        out_shape=out_shapes,
        scratch_shapes=[pltpu.VMEM((tm, d), BF16)],
        compiler_params=_params(("parallel", "arbitrary"), vmem),
        name=name,
    )(x, gain.reshape(1, d), w, *aux)


def _gla_chunks(chains):
    c = chains[0]["q"].shape[0]
    n_levels = c.bit_length()
    for ch in chains:
        hi = ch["lf2"].astype(BF16)
        lo = (ch["lf2"] - hi.astype(F32)).astype(BF16)
        ch["s"] = jnp.dot(ch["sums"], jnp.concatenate([hi, lo], axis=0), preferred_element_type=F32)
    for ch in chains:
        qd = ch["q"] * jnp.exp2(ch["s"][:c]).astype(BF16)
        ch["o"] = lax.dot_general(qd, ch["st"].astype(BF16), _NT, preferred_element_type=F32)
        ch["a"] = jnp.zeros((c, 2 * c), F32)
    for level in range(0, n_levels, 2):
        for ch in chains:
            qs, ks = [], []
            for l in (level, level + 1):
                if l == 0:
                    qs.append(ch["q"])
                    ks.append(ch["k"])
                elif l < n_levels:
                    e = jnp.exp2(-jnp.abs(ch["s"][l * c:(l + 1) * c])).astype(BF16)
                    qs.append(ch["q"] * e)
                    ks.append(ch["k"] * e)
            if len(ks) == 1:
                ks = ks * 2
            p = lax.dot_general(jnp.concatenate(qs, axis=0), jnp.concatenate(ks, axis=0), _NT,
                                preferred_element_type=F32)
            for i in range(len(qs)):
                ch["a"] = jnp.where(ch["lvl"] == 2 * (level + i) + i, p[i * c:(i + 1) * c, :], ch["a"])
    outs = []
    for ch in chains:
        g = ch["s"][:c]
        g_tot = g[c - 1:c, :] if ch["fwd"] else g[0:1, :]
        vv = jnp.concatenate([ch["v"], ch["v"]], axis=0)
        o = ch["o"] + jnp.dot(ch["a"].astype(BF16), vv, preferred_element_type=F32)
        kd = ch["k"] * jnp.exp2(g_tot - g).astype(BF16)
        st_new = (ch["st"] * jnp.exp2(g_tot)
                  + lax.dot_general(ch["v"], kd, _TN, preferred_element_type=F32))
        outs.append((o, st_new))
    return outs


def _gla_tables(c, fwd):
    ti = lax.broadcasted_iota(jnp.int32, (c, c), 0)
    ui = lax.broadcasted_iota(jnp.int32, (c, c), 1)

    def cum(row):
        return ((ui <= row) if fwd else (ui >= row)).astype(jnp.int32)

    blocks = [cum(ti)]
    half = 1
    while half < c:
        boundary = (ti & -(2 * half)) + (half - 1 if fwd else half)
        blocks.append(cum(ti) - cum(boundary))
        half *= 2
    x = ti ^ ui
    top_bit = sum((x >= (1 << b)).astype(jnp.int32) for b in range(1, c.bit_length() - 1))
    lvl = jnp.where(ti == ui, 0, jnp.where((ui < ti) if fwd else (ui > ti), 1 + top_bit, -1))
    sums = jnp.concatenate(blocks, axis=0).astype(F32).astype(BF16)
    codes = jnp.concatenate([2 * lvl, 2 * lvl + 1], axis=1)
    return jnp.concatenate([sums, sums], axis=1), codes


def _gla_body(qf_ref, kf_ref, vf_ref, lf_ref, qb_ref, kb_ref, vb_ref, lb_ref, of_ref, ob_ref,
              st_ref, sums_ref, lvl_ref, *, chunk):
    rows, width = qf_ref.shape
    nch = rows // chunk
    c = chunk

    @pl.when(pl.program_id(2) == 0)
    def _():
        st_ref[...] = jnp.zeros_like(st_ref)

    dirs = []
    for d, (fwd, refs) in enumerate(((True, (qf_ref, kf_ref, vf_ref, lf_ref, of_ref)),
                                     (False, (qb_ref, kb_ref, vb_ref, lb_ref, ob_ref)))):
        sums, lvl = _gla_tables(c, fwd)
        sums_ref[d] = sums
        lvl_ref[d] = lvl
        dirs.append((fwd, refs))

    def one_chunk(ci, carry):
        chains, dests = [], []
        for d, (fwd, (q_ref, k_ref, v_ref, l_ref, o_ref)) in enumerate(dirs):
            r0 = pl.multiple_of((ci if fwd else nch - 1 - ci) * c, c)
            for h in range(width // HEAD_DIM):
                hs = slice(h * HEAD_DIM, (h + 1) * HEAD_DIM)
                chains.append(dict(q=q_ref[pl.ds(r0, c), hs], k=k_ref[pl.ds(r0, c), hs],
                                   v=v_ref[pl.ds(r0, c), hs], lf2=l_ref[pl.ds(r0, c), hs],
                                   st=st_ref[d, h], lvl=lvl_ref[d], fwd=fwd, sums=sums_ref[d]))
                dests.append((o_ref, r0, hs, d, h))
        for (o, st_new), (o_ref, r0, hs, d, h) in zip(_gla_chunks(chains), dests):
            o_ref[pl.ds(r0, c), hs] = o.astype(o_ref.dtype)
            st_ref[d, h] = st_new
        return carry

    lax.fori_loop(0, nch, one_chunk, 0)


def gla_bidirectional(qvg, kk, log2f, *, batch, seq, heads, heads_per_step=4, rows_per_step=512):
    t = batch * seq
    hg = heads_per_step
    width = hg * HEAD_DIM
    rb = rows_per_step
    ns = seq // rb
    ng = heads // hg

    def spec(fwd, seg):
        return pl.BlockSpec((rb, width),
                            lambda b, g, s: (b * ns + (s if fwd else ns - 1 - s), seg * ng + g))

    return pl.pallas_call(
        functools.partial(_gla_body, chunk=GLA_CHUNK),
        grid=(batch, ng, ns),
        in_specs=[spec(True, 0), spec(True, 0), spec(True, 1), spec(True, 0),
                  spec(False, 0), spec(False, 1), spec(False, 1), spec(False, 1)],
        out_specs=[spec(True, 0), spec(False, 0)],
        out_shape=[jax.ShapeDtypeStruct((t, heads * HEAD_DIM), BF16)] * 2,
        scratch_shapes=[pltpu.VMEM((2, hg, HEAD_DIM, HEAD_DIM), F32),
                        pltpu.VMEM((2, GLA_CHUNK * GLA_CHUNK.bit_length(), 2 * GLA_CHUNK), BF16),
                        pltpu.VMEM((2, GLA_CHUNK, 2 * GLA_CHUNK), jnp.int32)],
        compiler_params=_params(("parallel", "parallel", "arbitrary"), 32 << 20),
        name="gla_bidir",
    )(qvg, kk, qvg, log2f, qvg, kk, qvg, log2f)


def _proj_residual_body(*refs, n_pro, prologue):
    pro, (w_ref, x_ref, o_ref, y_ref) = refs[:n_pro], refs[n_pro:]

    @pl.when(pl.program_id(1) == 0)
    def _():
        prologue(pro, y_ref)

    o_ref[...] = x_ref[...] + jnp.dot(y_ref[...], w_ref[...], preferred_element_type=F32)


def proj_residual(name, pro_inputs, pro_specs, prologue, w, xres, tm=512, tn=512):
    m, n = xres.shape
    kdim = w.shape[0]
    pro_bytes = sum(2 * tm * kdim * jnp.dtype(a.dtype).itemsize for a in pro_inputs)
    vmem = pro_bytes + tm * kdim * 2 + 2 * kdim * tn * 2 + 4 * tm * tn * 4 + 4 * tm * kdim * 4 + (4 << 20)
    return pl.pallas_call(
        functools.partial(_proj_residual_body, n_pro=len(pro_inputs), prologue=prologue),
        grid=(m // tm, n // tn),
        in_specs=[*pro_specs,
                  pl.BlockSpec((kdim, tn), lambda i, j: (0, j)),
                  pl.BlockSpec((tm, tn), lambda i, j: (i, j))],
        out_specs=pl.BlockSpec((tm, tn), lambda i, j: (i, j)),
        out_shape=jax.ShapeDtypeStruct((m, n), F32),
        scratch_shapes=[pltpu.VMEM((tm, kdim), BF16)],
        compiler_params=_params(("parallel", "arbitrary"), vmem),
        name=name,
    )(*pro_inputs, w, xres)


def _hgrn_out_prologue(pro, y_ref):
    of_ref, ob_ref, gate_ref, gain_ref = pro
    for h in range(of_ref.shape[1] // HEAD_DIM):
        hs = slice(h * HEAD_DIM, (h + 1) * HEAD_DIM)
        o = of_ref[:, hs].astype(F32) + ob_ref[:, hs].astype(F32)
        y = _rms_rows(o, gain_ref[:, hs]) * gate_ref[:, hs].astype(F32)
        y_ref[:, hs] = y.astype(BF16)


def _copy_prologue(pro, y_ref):
    y_ref[...] = pro[0][...]


def _swiglu_body(x_ref, g_ref, wg_ref, wu_ref, wd_ref, *rest, n_cast):
    cast_in, o_ref, cast_out, hn_ref = rest[:n_cast], rest[n_cast], rest[n_cast + 1:-1], rest[-1]

    @pl.when(pl.program_id(1) == 0)
    def _():
        x = x_ref[...]
        hn_ref[...] = _rms_rows(x, g_ref[...]).astype(BF16)
        o_ref[...] = x

    hn = hn_ref[...]
    a = jnp.dot(hn, wg_ref[...], preferred_element_type=F32)
    u = jnp.dot(hn, wu_ref[...], preferred_element_type=F32)
    h = (_silu(a) * u).astype(BF16)
    o_ref[...] += jnp.dot(h, wd_ref[...], preferred_element_type=F32)
    for src, dst in zip(cast_in, cast_out):
        dst[...] = src[...].astype(dst.dtype)


def _cast_block_rows(rows, steps):
    br = 16
    while rows % br or rows // br > steps:
        br += 16
    return br


def norm_swiglu_residual(x, gain, wg, wu, wd, cast=(), tm=512, tf=512):
    m, d = x.shape
    f = wg.shape[1]
    assert m % tm == 0 and f % tf == 0
    nf = f // tf
    steps = (m // tm) * nf
    cast_specs = []
    for arr in cast:
        br = _cast_block_rows(arr.shape[0], steps)
        last = arr.shape[0] // br - 1
        cast_specs.append(pl.BlockSpec((br, arr.shape[1]),
                                       lambda i, j, last=last: (jnp.minimum(i * nf + j, last), 0)))
    cast_bytes = sum(2 * s.block_shape[0] * s.block_shape[1] * 6 for s in cast_specs)
    vmem = (4 * tm * d * 4 + tm * d * 2 + 2 * (2 * d * tf + tf * d) * 2 + 3 * tm * tf * 4 + 2 * tm * d * 4
            + cast_bytes + (4 << 20))
    outs = pl.pallas_call(
        functools.partial(_swiglu_body, n_cast=len(cast)),
        grid=(m // tm, nf),
        in_specs=[pl.BlockSpec((tm, d), lambda i, j: (i, 0)),
                  pl.BlockSpec((1, d), lambda i, j: (0, 0)),
                  pl.BlockSpec((d, tf), lambda i, j: (0, j)),
                  pl.BlockSpec((d, tf), lambda i, j: (0, j)),
                  pl.BlockSpec((tf, d), lambda i, j: (j, 0)),
                  *cast_specs],
        out_specs=[pl.BlockSpec((tm, d), lambda i, j: (i, 0)), *cast_specs],
        out_shape=[jax.ShapeDtypeStruct((m, d), F32),
                   *[jax.ShapeDtypeStruct(arr.shape, BF16) for arr in cast]],
        scratch_shapes=[pltpu.VMEM((tm, d), BF16)],
        compiler_params=_params(("arbitrary", "arbitrary"), vmem),
        name="dense_swiglu",
    )(x, gain.reshape(1, d), wg, wu, wd, *cast)
    return outs[0], outs[1:]


def _dilated_attn_body(*refs, hg, branches, tq):
    q_refs, k_refs, v_refs = (refs[i * hg:(i + 1) * hg] for i in range(3))
    o_ref = refs[3 * hg]
    acc_refs, m_refs, l_refs = (refs[3 * hg + 1 + i * hg:3 * hg + 1 + (i + 1) * hg] for i in range(3))
    bias_ref = refs[6 * hg + 1]
    seq = q_refs[0].shape[0]
    steps = branches[0][1]
    tq_max = min(tq, seq)
    wk_max = min(seq, tq_max + 2 * steps)

    d = (lax.broadcasted_iota(jnp.int32, (tq_max, wk_max), 1)
         - lax.broadcasted_iota(jnp.int32, (tq_max, wk_max), 0))
    for shift in range(3):
        bias_ref[shift] = jnp.where(jnp.abs(d - shift * steps) <= steps, 0.0, NEG_INF)

    for bi, (r, br_steps) in enumerate(branches):
        assert br_steps == steps
        n = seq // r
        tqb = min(tq, n)
        wk = min(n, tqb + 2 * steps)
        nqb = n // tqb

        def block(idx, carry, bi=bi, r=r, n=n, tqb=tqb, wk=wk, nqb=nqb):
            rho = idx // nqb
            q0 = (idx - rho * nqb) * tqb
            k0 = jnp.clip(q0 - steps, 0, n - wk)
            bias = bias_ref[(q0 - k0) // steps, :tqb, :wk]

            def rows(c0, cnt):
                if r == 1:
                    return pl.ds(pl.multiple_of(c0, steps), cnt)
                return pl.ds(rho + r * c0, cnt, stride=r)

            s = [lax.dot_general(q_refs[h][rows(q0, tqb), :].astype(BF16),
                                 k_refs[h][rows(k0, wk), :].astype(BF16), _NT,
                                 preferred_element_type=F32) + bias for h in range(hg)]
            m_new = [jnp.broadcast_to(jnp.max(s[h], axis=-1, keepdims=True), (tqb, HEAD_DIM))
                     for h in range(hg)]
            p = [jnp.exp(s[h] - m_new[h][:, :1]) for h in range(hg)]
            l_new = [jnp.broadcast_to(jnp.sum(p[h], axis=-1, keepdims=True), (tqb, HEAD_DIM))
                     for h in range(hg)]
            acc_new = [jnp.dot(p[h].astype(BF16), v_refs[h][rows(k0, wk), :].astype(BF16),
                               preferred_element_type=F32) for h in range(hg)]
            for h in range(hg):
                if bi > 0:
                    m_old = m_refs[h][rows(q0, tqb), :]
                    m_all = jnp.maximum(m_old, m_new[h])
                    w_old = jnp.exp(m_old - m_all)
                    w_new = jnp.exp(m_new[h] - m_all)
                    acc_new[h] = acc_refs[h][rows(q0, tqb), :] * w_old + acc_new[h] * w_new
                    l_new[h] = l_refs[h][rows(q0, tqb), :] * w_old + l_new[h] * w_new
                    m_new[h] = m_all
                acc_refs[h][rows(q0, tqb), :] = acc_new[h]
                l_refs[h][rows(q0, tqb), :] = l_new[h]
                m_refs[h][rows(q0, tqb), :] = m_new[h]
            return carry

        lax.fori_loop(0, r * nqb, block, 0)

    for h in range(hg):
        o_ref[:, h * HEAD_DIM:(h + 1) * HEAD_DIM] = (acc_refs[h][...] / l_refs[h][...]).astype(o_ref.dtype)


def dilated_attention(qkv, *, batch, seq, heads, heads_per_step=2, tq=256):
    t = batch * seq
    hd = HEAD_DIM
    hg = heads_per_step
    branches = tuple((dil, window // (2 * dil)) for window, dil in DIL_BRANCHES)
    steps = branches[0][1]
    tq_max = min(tq, seq)
    specs = lambda part: [pl.BlockSpec((seq, hd), lambda b, g, h=h: (b, part * heads + g * hg + h))
                          for h in range(hg)]
    return pl.pallas_call(
        functools.partial(_dilated_attn_body, hg=hg, branches=branches, tq=tq),
        grid=(batch, heads // hg),
        in_specs=specs(0) + specs(1) + specs(2),
        out_specs=pl.BlockSpec((seq, hg * hd), lambda b, g: (b, g)),
        out_shape=jax.ShapeDtypeStruct((t, heads * hd), BF16),
        scratch_shapes=([pltpu.VMEM((seq, hd), F32)] * (3 * hg)
                        + [pltpu.VMEM((3, tq_max, min(seq, tq_max + 2 * steps)), F32)]),
        compiler_params=_params(("parallel", "parallel"),
                                hg * seq * hd * (2 * 3 * 4 + 2 * 2 + 3 * 4) + (16 << 20)),
        name="dilated_attn",
    )(*([qkv] * (3 * hg)))


def _router_body(x_ref, g_ref, wr_ref, hn_ref, e_ref, p_ref):
    hn = _rms_rows(x_ref[...], g_ref[...])
    hn_ref[...] = hn
    logits = jnp.dot(hn, wr_ref[...], precision=lax.Precision.HIGHEST, preferred_element_type=F32)
    lane = lax.broadcasted_iota(jnp.int32, logits.shape, 1)
    logits = jnp.where(lane < N_EXPERTS, logits, -jnp.inf)
    m1 = jnp.max(logits, axis=-1, keepdims=True)
    i1 = jnp.min(jnp.where(logits == m1, lane, LANES), axis=-1, keepdims=True)
    rest = jnp.where(lane == i1, -jnp.inf, logits)
    m2 = jnp.max(rest, axis=-1, keepdims=True)
    i2 = jnp.min(jnp.where(rest == m2, lane, LANES), axis=-1, keepdims=True)
    e2 = jnp.exp(m2 - m1)
    den = 1.0 + e2
    e_ref[...] = jnp.where(lane == 0, i1, jnp.where(lane == 1, i2, 0))
    p_ref[...] = jnp.where(lane == 0, 1.0 / den, jnp.where(lane == 1, e2 / den, 0.0))


def moe_router(x, gain, w_router_padded, tm=512):
    m, d = x.shape
    return pl.pallas_call(
        _router_body,
        grid=(m // tm,),
        in_specs=[pl.BlockSpec((tm, d), lambda i: (i, 0)),
                  pl.BlockSpec((1, d), lambda i: (0, 0)),
                  pl.BlockSpec((d, LANES), lambda i: (0, 0))],
        out_specs=[pl.BlockSpec((tm, d), lambda i: (i, 0)),
                   pl.BlockSpec((tm, LANES), lambda i: (i, 0)),
                   pl.BlockSpec((tm, LANES), lambda i: (i, 0))],
        out_shape=[jax.ShapeDtypeStruct((m, d), F32),
                   jax.ShapeDtypeStruct((m, LANES), jnp.int32),
                   jax.ShapeDtypeStruct((m, LANES), F32)],
        compiler_params=_params(("parallel",), 6 * tm * d * 4 + (8 << 20)),
        name="moe_router",
    )(x, gain.reshape(1, d), w_router_padded)


def _gather_rows_per_step(tb, nf):
    per_step = -(-tb // nf)
    while (per_step * nf) % 8:
        per_step += 1
    return per_step


def _expert_body(blk_e_ref, n_used_ref, tok_ref, tok_next_ref, hn_hbm, wg_ref, wu_ref, wd_ref,
                 y_ref, xf_ref, xb_ref, sem, *, nf, nblk):
    b = pl.program_id(0)
    f = pl.program_id(1)
    tb = y_ref.shape[0]
    n_used = n_used_ref[0]
    used = b < n_used
    slot = lax.rem(b, 2)
    per_step = _gather_rows_per_step(tb, nf)

    def row_copy(toks, j, dst_slot):
        return pltpu.make_async_copy(hn_hbm.at[pl.ds(toks[0, jnp.minimum(j, tb - 1)], 1), :],
                                     xf_ref.at[dst_slot, pl.ds(j, 1), :], sem.at[dst_slot])

    def wait_gather(dst_slot):
        rows = per_step * nf
        pltpu.make_async_copy(hn_hbm.at[pl.ds(0, rows), :], xf_ref.at[dst_slot], sem.at[dst_slot]).wait()

    @pl.when((b == 0) & (f == 0))
    def _():
        def start(j, c):
            row_copy(tok_ref, j, 0).start()
            return c
        lax.fori_loop(0, per_step * nf, start, 0)

    @pl.when(used & (f == 0))
    def _():
        wait_gather(slot)
        xb_ref[...] = xf_ref[slot, :tb, :].astype(BF16)
        y_ref[...] = jnp.zeros_like(y_ref)

    @pl.when(~used & (f == 0))
    def _():
        y_ref[...] = jnp.zeros_like(y_ref)

    @pl.when(used)
    def _():
        for j in range(per_step):
            row_copy(tok_next_ref, f * per_step + j, 1 - slot).start()
        xb = xb_ref[...]
        a = jnp.dot(xb, wg_ref[...], preferred_element_type=F32)
        u = jnp.dot(xb, wu_ref[...], preferred_element_type=F32)
        h = (_silu(a) * u).astype(BF16)
        y_ref[...] += jnp.dot(h, wd_ref[...], preferred_element_type=F32)

    @pl.when(used & (f == nf - 1) & (b + 1 >= n_used))
    def _():
        wait_gather(1 - slot)


def moe_experts(hn, slot_tok, blk_e, n_used, wg, wu, wd, tb, tf=512):
    t, d = hn.shape
    p = slot_tok.shape[0]
    nblk = p // tb
    fdim = wg.shape[2]
    nf = fdim // tf

    def live(b, n_used_ref):
        return jnp.minimum(b, n_used_ref[0] - 1)

    def wmap_cols(b, f, blk_e_ref, n_used_ref):
        return (blk_e_ref[live(b, n_used_ref)], 0, jnp.where(b < n_used_ref[0], f, nf - 1))

    def wmap_rows(b, f, blk_e_ref, n_used_ref):
        return (blk_e_ref[live(b, n_used_ref)], jnp.where(b < n_used_ref[0], f, nf - 1), 0)

    vmem = (tb * d * (2 * 4 + 2) + 2 * tb * d * 4 + 2 * (2 * d * tf + tf * d) * 2 + 3 * tb * tf * 4
            + tb * d * 4 + (4 << 20))
    toks = slot_tok.reshape(nblk, 1, tb)
    grid_spec = pltpu.PrefetchScalarGridSpec(
        num_scalar_prefetch=2,
        grid=(nblk, nf),
        in_specs=[pl.BlockSpec((None, 1, tb), lambda b, f, *_: (b, 0, 0), memory_space=pltpu.SMEM),
                  pl.BlockSpec((None, 1, tb), lambda b, f, *_: (jnp.minimum(b + 1, nblk - 1), 0, 0),
                               memory_space=pltpu.SMEM),
                  pl.BlockSpec(memory_space=pl.ANY),
                  pl.BlockSpec((None, d, tf), wmap_cols),
                  pl.BlockSpec((None, d, tf), wmap_cols),
                  pl.BlockSpec((None, tf, d), wmap_rows)],
        out_specs=pl.BlockSpec((tb, d), lambda b, f, *_: (b, 0)),
        scratch_shapes=[pltpu.VMEM((2, _gather_rows_per_step(tb, nf) * nf, d), F32), pltpu.VMEM((tb, d), BF16),
                        pltpu.SemaphoreType.DMA((2,))],
    )
    return pl.pallas_call(
        functools.partial(_expert_body, nf=nf, nblk=nblk),
        grid_spec=grid_spec,
        out_shape=jax.ShapeDtypeStruct((p, d), F32),
        compiler_params=_params(("arbitrary", "arbitrary"), vmem),
        name="moe_experts",
    )(blk_e, n_used, toks, toks, hn, wg, wu, wd)


def _combine_body(slot_ref, slot_next_ref, x_ref, p_ref, y_hbm, o_ref, buf_ref, sem):
    i = pl.program_id(0)
    nt = pl.num_programs(0)
    tm = x_ref.shape[0]
    cur = lax.rem(i, 2)

    def row_copy(slots, j, half):
        k, r = divmod(j, tm)
        return pltpu.make_async_copy(y_hbm.at[pl.ds(slots[0, j], 1), :],
                                     buf_ref.at[half, k, pl.ds(r, 1), :], sem.at[half])

    def wait_tile(half):
        for k in range(TOP_K):
            pltpu.make_async_copy(y_hbm.at[pl.ds(0, tm), :], buf_ref.at[half, k], sem.at[half]).wait()

    @pl.when(i == 0)
    def _():
        for j in range(TOP_K * tm):
            row_copy(slot_ref, j, 0).start()

    for j in range(TOP_K * tm):
        row_copy(slot_next_ref, j, 1 - cur).start()
    wait_tile(cur)
    gates = p_ref[...]
    o_ref[...] = x_ref[...] + (gates[:, 0:1] * buf_ref[cur, 0] + gates[:, 1:2] * buf_ref[cur, 1])

    @pl.when(i == nt - 1)
    def _():
        wait_tile(1 - cur)


def moe_combine(x, y, tok_slots, gates, tm=256):
    t, d = x.shape
    nt = t // tm
    slots = tok_slots.reshape(nt, tm, TOP_K).transpose(0, 2, 1).reshape(nt, 1, TOP_K * tm)
    slot_spec = lambda index_map: pl.BlockSpec((None, 1, TOP_K * tm), index_map, memory_space=pltpu.SMEM)
    return pl.pallas_call(
        _combine_body,
        grid=(nt,),
        in_specs=[slot_spec(lambda i: (i, 0, 0)),
                  slot_spec(lambda i: (jnp.minimum(i + 1, nt - 1), 0, 0)),
                  pl.BlockSpec((tm, d), lambda i: (i, 0)),
                  pl.BlockSpec((tm, LANES), lambda i: (i, 0)),
                  pl.BlockSpec(memory_space=pl.ANY)],
        out_specs=pl.BlockSpec((tm, d), lambda i: (i, 0)),
        out_shape=jax.ShapeDtypeStruct((t, d), F32),
        scratch_shapes=[pltpu.VMEM((2, TOP_K, tm, d), F32), pltpu.SemaphoreType.DMA((2,))],
        compiler_params=_params(("arbitrary",), (4 + 2 * TOP_K) * tm * d * 4 + (4 << 20)),
        name="moe_combine",
    )(slots, slots, x, gates, y)


def _routing_plan(top_e, tb):
    t = top_e.shape[0]
    n = t * TOP_K
    e_flat = top_e.reshape(n)
    onehot = (e_flat[:, None] == jnp.arange(N_EXPERTS, dtype=jnp.int32)[None, :]).astype(jnp.int32)
    rank = jnp.take_along_axis(jnp.cumsum(onehot, axis=0) - onehot, e_flat[:, None], axis=1)[:, 0]
    counts = jnp.sum(onehot, axis=0)
    padded = (counts + tb - 1) // tb * tb
    pend = jnp.cumsum(padded)
    dest = (pend - padded)[e_flat] + rank
    nblk = -(-n // tb) + N_EXPERTS
    p = nblk * tb
    tok_flat = jnp.arange(n, dtype=jnp.int32) // TOP_K
    slot_tok = jnp.zeros((p,), jnp.int32).at[dest].set(tok_flat)
    blk_start = jnp.arange(nblk, dtype=pend.dtype) * tb
    blk_e = jnp.minimum(jnp.searchsorted(pend, blk_start, side='right'), N_EXPERTS - 1).astype(jnp.int32)
    n_used = (pend[-1] // tb).astype(jnp.int32).reshape(1)
    return slot_tok, blk_e, n_used, dest.reshape(t, TOP_K).astype(jnp.int32)


def _rope_tables(seq, hd):
    half = hd // 2
    inv_freq = ROPE_THETA ** (-jnp.arange(half, dtype=F32) * 2.0 / hd)
    ang = jnp.arange(seq, dtype=F32)[:, None] * inv_freq[None, :]
    cos, sin = jnp.cos(ang), jnp.sin(ang)
    return jnp.concatenate([cos, cos], axis=-1), jnp.concatenate([-sin, sin], axis=-1)


def kernel(x, norm_gains, hgrn_w_in, hgrn_lb_logits, hgrn_onorm, hgrn_w_out, attn_w_qkv, attn_q_gain,
           attn_k_gain, attn_w_out, ffn_w_gate, ffn_w_up, ffn_w_down, moe_w_router, moe_w_gate, moe_w_up,
           moe_w_down):
    batch, seq, d = x.shape
    t = batch * seq
    heads = d // HEAD_DIM
    hd = HEAD_DIM
    xf = x.reshape(t, d)

    w_in = hgrn_w_in[0]
    w_qvg = jnp.concatenate([w_in[:, :d], w_in[:, 3 * d:5 * d]], axis=1).astype(BF16)
    w_f = w_in[:, d:3 * d].astype(BF16)
    lb = jnp.cumsum(jax.nn.softmax(hgrn_lb_logits.astype(F32), axis=0), axis=0)[0].reshape(1, 2 * d)
    tn = min(1024, d)
    nq = d // tn

    def qvg_epilogue(acc, j, aux, outs, cols):
        is_v = (j >= nq) & (j < 2 * nq)
        outs[0][:, cols] = jnp.where(is_v, acc, _silu(acc)).astype(BF16)

    (qvg,) = norm_matmul(
        "hgrn_qvg_proj", xf, norm_gains[0, 0], w_qvg, qvg_epilogue,
        [jax.ShapeDtypeStruct((t, 3 * d), BF16)], [pl.BlockSpec((1024, tn), lambda i, j: (i, j))], tn=tn)

    def f_epilogue(acc, j, aux, outs, cols):
        lbv = aux[0][:, cols]
        fgate = lbv + (1.0 - lbv) * jax.nn.sigmoid(acc)
        outs[0][:, cols] = jnp.log2(fgate)
        outs[1][:, cols] = (1.0 - fgate).astype(BF16)

    log2f, kk = norm_matmul(
        "hgrn_forget_proj", xf, norm_gains[0, 0], w_f, f_epilogue,
        [jax.ShapeDtypeStruct((t, 2 * d), F32), jax.ShapeDtypeStruct((t, 2 * d), BF16)],
        [pl.BlockSpec((1024, tn), lambda i, j: (i, j)), pl.BlockSpec((1024, tn), lambda i, j: (i, j))],
        aux=(lb,), aux_specs=(pl.BlockSpec((1, tn), lambda i, j: (0, j)),), tn=tn)

    o_f, o_b = gla_bidirectional(qvg, kk, log2f, batch=batch, seq=seq, heads=heads)

    tm = 512
    x1 = proj_residual(
        "hgrn_out_proj", (o_f, o_b, qvg, hgrn_onorm[0].reshape(1, d)),
        (pl.BlockSpec((tm, d), lambda i, j: (i, 0)), pl.BlockSpec((tm, d), lambda i, j: (i, 0)),
         pl.BlockSpec((tm, d), lambda i, j: (i, 2)), pl.BlockSpec((1, d), lambda i, j: (0, 0))),
        _hgrn_out_prologue, hgrn_w_out[0].astype(BF16), xf, tm=tm)

    fdim = ffn_w_gate.shape[2]
    fpad = -(-fdim // 512) * 512 - fdim
    wg = jnp.pad(ffn_w_gate[0], ((0, 0), (0, fpad))).astype(BF16)
    wu = jnp.pad(ffn_w_up[0], ((0, 0), (0, fpad))).astype(BF16)
    wd = jnp.pad(ffn_w_down[0], ((0, fpad), (0, 0))).astype(BF16)
    n_exp, _, f_exp = moe_w_gate.shape[1:]
    later_f32 = (moe_w_gate[0].reshape(n_exp * d, f_exp), moe_w_up[0].reshape(n_exp * d, f_exp),
                 moe_w_down[0].reshape(n_exp * f_exp, d), attn_w_qkv[0], attn_w_out[0])
    x2, (moe_wg, moe_wu, moe_wd, w_qkv, w_attn_out) = norm_swiglu_residual(
        x1, norm_gains[0, 1], wg, wu, wd, cast=later_f32)

    cos, sin = _rope_tables(seq, hd)
    qg = attn_q_gain[0].reshape(1, hd)
    kg = attn_k_gain[0].reshape(1, hd)
    tm_qkv = 1024
    tn_qkv = 512
    nq_qkv = d // tn_qkv
    pos_blocks = seq // tm_qkv

    def qkv_epilogue(acc, j, aux, outs, cols):
        cos_ref, sin_ref, qg_ref, kg_ref = aux
        o_ref = outs[0]

        def normed_rope(gain, scale):
            for h in range(acc.shape[1] // hd):
                y = _rms_rows(acc[:, h * hd:(h + 1) * hd], gain)
                y = y * cos_ref[...] + pltpu.roll(y, hd // 2, 1) * sin_ref[...]
                o_ref[:, cols.start + h * hd:cols.start + (h + 1) * hd] = y * scale

        @pl.when(j < nq_qkv)
        def _():
            normed_rope(qg_ref[...], hd ** -0.5)

        @pl.when((j >= nq_qkv) & (j < 2 * nq_qkv))
        def _():
            normed_rope(kg_ref[...], 1.0)

        @pl.when(j >= 2 * nq_qkv)
        def _():
            o_ref[:, cols] = acc

    (qkv,) = norm_matmul(
        "attn_qkv_proj", x2, norm_gains[1, 0], w_qkv, qkv_epilogue,
        [jax.ShapeDtypeStruct((t, 3 * d), F32)], [pl.BlockSpec((tm_qkv, tn_qkv), lambda i, j: (i, j))],
        aux=(cos, sin, qg, kg),
        aux_specs=(pl.BlockSpec((tm_qkv, hd), lambda i, j: (i % pos_blocks, 0)),
                   pl.BlockSpec((tm_qkv, hd), lambda i, j: (i % pos_blocks, 0)),
                   pl.BlockSpec((1, hd), lambda i, j: (0, 0)),
                   pl.BlockSpec((1, hd), lambda i, j: (0, 0))),
        tm=tm_qkv, tn=tn_qkv, sub=tn_qkv)

    attn = dilated_attention(qkv, batch=batch, seq=seq, heads=heads)
    x3 = proj_residual("attn_out_proj", (attn,), (pl.BlockSpec((tm, d), lambda i, j: (i, 0)),),
                       _copy_prologue, w_attn_out, x2, tm=tm)

    tb = 512
    w_router = jnp.pad(moe_w_router[0].astype(F32), ((0, 0), (0, LANES - N_EXPERTS)))
    hn3, e_pad, p_pad = moe_router(x3, norm_gains[1, 1], w_router)
    slot_tok, blk_e, n_used, tok_slots = _routing_plan(e_pad[:, :TOP_K], tb)
    y = moe_experts(hn3, slot_tok, blk_e, n_used, moe_wg.reshape(n_exp, d, f_exp),
                    moe_wu.reshape(n_exp, d, f_exp), moe_wd.reshape(n_exp, f_exp, d), tb)
    out = moe_combine(x3, y, tok_slots, p_pad)
    return out.reshape(batch, seq, d)
```

```python
import functools

import jax
import jax.numpy as jnp
from jax import lax
from jax.experimental import pallas as pl
from jax.experimental.pallas import tpu as pltpu

F32 = jnp.float32
BF16 = jnp.bfloat16
EPS = 1e-6
NEG_INF = -1e30
ROPE_THETA = 10000.0

HEAD_DIM = 128
GLA_CHUNK = 64
DIL_BRANCHES = ((128, 1), (512, 4), (2048, 16))
N_EXPERTS = 8
TOP_K = 2

LANES = 128
V7X_VMEM_BYTES = 64 * 1024 * 1024
VMEM_BUDGET = 56 * 1024 * 1024

_NT = (((1,), (1,)), ((), ()))
_TN = (((0,), (0,)), ((), ()))


def _params(semantics, vmem_bytes):
    return pltpu.CompilerParams(dimension_semantics=semantics,
                                vmem_limit_bytes=int(min(vmem_bytes, VMEM_BUDGET)))


def _silu(x):
    return x * jax.nn.sigmoid(x)


def _rms_rows(x, gain):
    ms = jnp.mean(x * x, axis=-1, keepdims=True)
    return x * lax.rsqrt(ms + EPS) * gain


def _norm_matmul_body(x_ref, g_ref, w_ref, *rest, n_aux, epilogue, sub):
    aux, outs, hn_ref = rest[:n_aux], rest[n_aux:-1], rest[-1]
    j = pl.program_id(1)

    @pl.when(j == 0)
    def _():
        hn_ref[...] = _rms_rows(x_ref[...], g_ref[...]).astype(BF16)

    for s in range(w_ref.shape[1] // sub):
        cols = slice(s * sub, (s + 1) * sub)
        acc = jnp.dot(hn_ref[...], w_ref[:, cols], preferred_element_type=F32)
        epilogue(acc, j, aux, outs, cols)


def norm_matmul(name, x, gain, w, epilogue, out_shapes, out_specs, aux=(), aux_specs=(), tm=1024, tn=512,
                sub=256):
    m, d = x.shape
    n = w.shape[1]
    assert m % tm == 0 and n % tn == 0
    out_bytes = sum(2 * tm * tn * jnp.dtype(s.dtype).itemsize for s in out_shapes)
    vmem = 2 * tm * d * 4 + tm * d * 2 + 2 * d * tn * 2 + out_bytes + 4 * tm * tn * 4 + (4 << 20)
    return pl.pallas_call(
        functools.partial(_norm_matmul_body, n_aux=len(aux), epilogue=epilogue, sub=min(sub, tn)),
        grid=(m // tm, n // tn),
        in_specs=[pl.BlockSpec((tm, d), lambda i, j: (i, 0)),
                  pl.BlockSpec((1, d), lambda i, j: (0, 0)),
                  pl.BlockSpec((d, tn), lambda i, j: (0, j)),
                  *aux_specs],
        out_specs=out_specs,
        out_shape=out_shapes,
        scratch_shapes=[pltpu.VMEM((tm, d), BF16)],
        compiler_params=_params(("parallel", "arbitrary"), vmem),
        name=name,
    )(x, gain.reshape(1, d), w, *aux)


def _gla_chunks(chains):
    c = chains[0]["q"].shape[0]
    n_levels = c.bit_length()
    for ch in chains:
        hi = ch["lf2"].astype(BF16)
        lo = (ch["lf2"] - hi.astype(F32)).astype(BF16)
        ch["s"] = jnp.dot(ch["sums"], jnp.concatenate([hi, lo], axis=0), preferred_element_type=F32)
    for ch in chains:
        qd = ch["q"] * jnp.exp2(ch["s"][:c]).astype(BF16)
        ch["o"] = lax.dot_general(qd, ch["st"].astype(BF16), _NT, preferred_element_type=F32)
        ch["a"] = jnp.zeros((c, 2 * c), F32)
    for level in range(0, n_levels, 2):
        for ch in chains:
            qs, ks = [], []
            for l in (level, level + 1):
                if l == 0:
                    qs.append(ch["q"])
                    ks.append(ch["k"])
                elif l < n_levels:
                    e = jnp.exp2(-jnp.abs(ch["s"][l * c:(l + 1) * c])).astype(BF16)
                    qs.append(ch["q"] * e)
                    ks.append(ch["k"] * e)
            if len(ks) == 1:
                ks = ks * 2
            p = lax.dot_general(jnp.concatenate(qs, axis=0), jnp.concatenate(ks, axis=0), _NT,
                                preferred_element_type=F32)
            for i in range(len(qs)):
                ch["a"] = jnp.where(ch["lvl"] == 2 * (level + i) + i, p[i * c:(i + 1) * c, :], ch["a"])
    outs = []
    for ch in chains:
        g = ch["s"][:c]
        g_tot = g[c - 1:c, :] if ch["fwd"] else g[0:1, :]
        vv = jnp.concatenate([ch["v"], ch["v"]], axis=0)
        o = ch["o"] + jnp.dot(ch["a"].astype(BF16), vv, preferred_element_type=F32)
        kd = ch["k"] * jnp.exp2(g_tot - g).astype(BF16)
        st_new = (ch["st"] * jnp.exp2(g_tot)
                  + lax.dot_general(ch["v"], kd, _TN, preferred_element_type=F32))
        outs.append((o, st_new))
    return outs


def _gla_tables(c, fwd):
    ti = lax.broadcasted_iota(jnp.int32, (c, c), 0)
    ui = lax.broadcasted_iota(jnp.int32, (c, c), 1)

    def cum(row):
        return ((ui <= row) if fwd else (ui >= row)).astype(jnp.int32)

    blocks = [cum(ti)]
    half = 1
    while half < c:
        boundary = (ti & -(2 * half)) + (half - 1 if fwd else half)
        blocks.append(cum(ti) - cum(boundary))
        half *= 2
    x = ti ^ ui
    top_bit = sum((x >= (1 << b)).astype(jnp.int32) for b in range(1, c.bit_length() - 1))
    lvl = jnp.where(ti == ui, 0, jnp.where((ui < ti) if fwd else (ui > ti), 1 + top_bit, -1))
    sums = jnp.concatenate(blocks, axis=0).astype(F32).astype(BF16)
    codes = jnp.concatenate([2 * lvl, 2 * lvl + 1], axis=1)
    return jnp.concatenate([sums, sums], axis=1), codes


def _gla_body(qf_ref, kf_ref, vf_ref, lf_ref, qb_ref, kb_ref, vb_ref, lb_ref, *rest, chunk, n_cast):
    cast_in, (of_ref, ob_ref) = rest[:n_cast], rest[n_cast:n_cast + 2]
    cast_out, (st_ref, sums_ref, lvl_ref) = rest[n_cast + 2:2 * n_cast + 2], rest[2 * n_cast + 2:]
    rows, width = qf_ref.shape
    nch = rows // chunk
    c = chunk

    for src, dst in zip(cast_in, cast_out):
        dst[...] = src[...].astype(dst.dtype)

    @pl.when(pl.program_id(2) == 0)
    def _():
        st_ref[...] = jnp.zeros_like(st_ref)

    dirs = []
    for d, (fwd, refs) in enumerate(((True, (qf_ref, kf_ref, vf_ref, lf_ref, of_ref)),
                                     (False, (qb_ref, kb_ref, vb_ref, lb_ref, ob_ref)))):
        sums, lvl = _gla_tables(c, fwd)
        sums_ref[d] = sums
        lvl_ref[d] = lvl
        dirs.append((fwd, refs))

    def one_chunk(ci, carry):
        chains, dests = [], []
        for d, (fwd, (q_ref, k_ref, v_ref, l_ref, o_ref)) in enumerate(dirs):
            r0 = pl.multiple_of((ci if fwd else nch - 1 - ci) * c, c)
            for h in range(width // HEAD_DIM):
                hs = slice(h * HEAD_DIM, (h + 1) * HEAD_DIM)
                chains.append(dict(q=q_ref[pl.ds(r0, c), hs], k=k_ref[pl.ds(r0, c), hs],
                                   v=v_ref[pl.ds(r0, c), hs], lf2=l_ref[pl.ds(r0, c), hs],
                                   st=st_ref[d, h], lvl=lvl_ref[d], fwd=fwd, sums=sums_ref[d]))
                dests.append((o_ref, r0, hs, d, h))
        for (o, st_new), (o_ref, r0, hs, d, h) in zip(_gla_chunks(chains), dests):
            o_ref[pl.ds(r0, c), hs] = o.astype(o_ref.dtype)
            st_ref[d, h] = st_new
        return carry

    lax.fori_loop(0, nch, one_chunk, 0)


def _cast_block_rows(rows, steps):
    br = 16
    while rows % br or rows // br > steps:
        br += 16
    return br


def gla_bidirectional(qvg, kk, log2f, *, batch, seq, heads, cast=(), heads_per_step=4, rows_per_step=512):
    t = batch * seq
    hg = heads_per_step
    width = hg * HEAD_DIM
    rb = rows_per_step
    ns = seq // rb
    ng = heads // hg

    def spec(fwd, seg):
        return pl.BlockSpec((rb, width),
                            lambda b, g, s: (b * ns + (s if fwd else ns - 1 - s), seg * ng + g))

    cast_specs = []
    for arr in cast:
        br = _cast_block_rows(arr.shape[0], batch * ng * ns)
        last = arr.shape[0] // br - 1
        cast_specs.append(pl.BlockSpec(
            (br, arr.shape[1]), lambda b, g, s, last=last: (jnp.minimum((b * ng + g) * ns + s, last), 0)))
    cast_bytes = sum(2 * s.block_shape[0] * s.block_shape[1] * (4 + 2) for s in cast_specs)
    outs = pl.pallas_call(
        functools.partial(_gla_body, chunk=GLA_CHUNK, n_cast=len(cast)),
        grid=(batch, ng, ns),
        in_specs=[spec(True, 0), spec(True, 0), spec(True, 1), spec(True, 0),
                  spec(False, 0), spec(False, 1), spec(False, 1), spec(False, 1), *cast_specs],
        out_specs=[spec(True, 0), spec(False, 0), *cast_specs],
        out_shape=[jax.ShapeDtypeStruct((t, heads * HEAD_DIM), BF16)] * 2
                  + [jax.ShapeDtypeStruct(arr.shape, BF16) for arr in cast],
        scratch_shapes=[pltpu.VMEM((2, hg, HEAD_DIM, HEAD_DIM), F32),
                        pltpu.VMEM((2, GLA_CHUNK * GLA_CHUNK.bit_length(), 2 * GLA_CHUNK), BF16),
                        pltpu.VMEM((2, GLA_CHUNK, 2 * GLA_CHUNK), jnp.int32)],
        compiler_params=_params(("arbitrary", "arbitrary", "arbitrary"), cast_bytes + (24 << 20)),
        name="gla_bidir",
    )(qvg, kk, qvg, log2f, qvg, kk, qvg, log2f, *cast)
    return outs[0], outs[1], outs[2:]


def _proj_residual_body(*refs, n_pro, prologue):
    pro, (w_ref, x_ref, o_ref, y_ref) = refs[:n_pro], refs[n_pro:]

    @pl.when(pl.program_id(1) == 0)
    def _():
        prologue(pro, y_ref)

    o_ref[...] = x_ref[...] + jnp.dot(y_ref[...], w_ref[...], preferred_element_type=F32)


def proj_residual(name, pro_inputs, pro_specs, prologue, w, xres, tm=512, tn=512):
    m, n = xres.shape
    kdim = w.shape[0]
    pro_bytes = sum(2 * tm * kdim * jnp.dtype(a.dtype).itemsize for a in pro_inputs)
    vmem = pro_bytes + tm * kdim * 2 + 2 * kdim * tn * 2 + 4 * tm * tn * 4 + 4 * tm * kdim * 4 + (4 << 20)
    return pl.pallas_call(
        functools.partial(_proj_residual_body, n_pro=len(pro_inputs), prologue=prologue),
        grid=(m // tm, n // tn),
        in_specs=[*pro_specs,
                  pl.BlockSpec((kdim, tn), lambda i, j: (0, j)),
                  pl.BlockSpec((tm, tn), lambda i, j: (i, j))],
        out_specs=pl.BlockSpec((tm, tn), lambda i, j: (i, j)),
        out_shape=jax.ShapeDtypeStruct((m, n), F32),
        scratch_shapes=[pltpu.VMEM((tm, kdim), BF16)],
        compiler_params=_params(("parallel", "arbitrary"), vmem),
        name=name,
    )(*pro_inputs, w, xres)


def _hgrn_out_prologue(pro, y_ref):
    of_ref, ob_ref, gate_ref, gain_ref = pro
    for h in range(of_ref.shape[1] // HEAD_DIM):
        hs = slice(h * HEAD_DIM, (h + 1) * HEAD_DIM)
        o = of_ref[:, hs].astype(F32) + ob_ref[:, hs].astype(F32)
        y = _rms_rows(o, gain_ref[:, hs]) * gate_ref[:, hs].astype(F32)
        y_ref[:, hs] = y.astype(BF16)


def _copy_prologue(pro, y_ref):
    y_ref[...] = pro[0][...]


def _swiglu_body(x_ref, g_ref, wg_ref, wu_ref, wd_ref, o_ref, hn_ref):
    @pl.when(pl.program_id(1) == 0)
    def _():
        x = x_ref[...]
        hn_ref[...] = _rms_rows(x, g_ref[...]).astype(BF16)
        o_ref[...] = x

    hn = hn_ref[...]
    a = jnp.dot(hn, wg_ref[...], preferred_element_type=F32)
    u = jnp.dot(hn, wu_ref[...], preferred_element_type=F32)
    h = (_silu(a) * u).astype(BF16)
    o_ref[...] += jnp.dot(h, wd_ref[...], preferred_element_type=F32)


def norm_swiglu_residual(x, gain, wg, wu, wd, tm=512, tf=512):
    m, d = x.shape
    f = wg.shape[1]
    assert m % tm == 0 and f % tf == 0
    vmem = 4 * tm * d * 4 + tm * d * 2 + 2 * (2 * d * tf + tf * d) * 2 + 3 * tm * tf * 4 + 2 * tm * d * 4 + (4 << 20)
    return pl.pallas_call(
        _swiglu_body,
        grid=(m // tm, f // tf),
        in_specs=[pl.BlockSpec((tm, d), lambda i, j: (i, 0)),
                  pl.BlockSpec((1, d), lambda i, j: (0, 0)),
                  pl.BlockSpec((d, tf), lambda i, j: (0, j)),
                  pl.BlockSpec((d, tf), lambda i, j: (0, j)),
                  pl.BlockSpec((tf, d), lambda i, j: (j, 0))],
        out_specs=pl.BlockSpec((tm, d), lambda i, j: (i, 0)),
        out_shape=jax.ShapeDtypeStruct((m, d), F32),
        scratch_shapes=[pltpu.VMEM((tm, d), BF16)],
        compiler_params=_params(("parallel", "arbitrary"), vmem),
        name="dense_swiglu",
    )(x, gain.reshape(1, d), wg, wu, wd)


def _dilated_attn_body(*refs, hg, branches, tq):
    q_refs, k_refs, v_refs = (refs[i * hg:(i + 1) * hg] for i in range(3))
    o_ref = refs[3 * hg]
    acc_refs, m_refs, l_refs = (refs[3 * hg + 1 + i * hg:3 * hg + 1 + (i + 1) * hg] for i in range(3))
    bias_ref = refs[6 * hg + 1]
    seq = q_refs[0].shape[0]
    steps = branches[0][1]
    tq_max = min(tq, seq)
    wk_max = min(seq, tq_max + 2 * steps)

    d = (lax.broadcasted_iota(jnp.int32, (tq_max, wk_max), 1)
         - lax.broadcasted_iota(jnp.int32, (tq_max, wk_max), 0))
    for shift in range(3):
        bias_ref[shift] = jnp.where(jnp.abs(d - shift * steps) <= steps, 0.0, NEG_INF)

    for bi, (r, br_steps) in enumerate(branches):
        assert br_steps == steps
        n = seq // r
        tqb = min(tq, n)
        wk = min(n, tqb + 2 * steps)
        nqb = n // tqb

        def block(idx, carry, bi=bi, r=r, n=n, tqb=tqb, wk=wk, nqb=nqb):
            rho = idx // nqb
            q0 = (idx - rho * nqb) * tqb
            k0 = jnp.clip(q0 - steps, 0, n - wk)
            bias = bias_ref[(q0 - k0) // steps, :tqb, :wk]

            def rows(c0, cnt):
                if r == 1:
                    return pl.ds(pl.multiple_of(c0, steps), cnt)
                return pl.ds(rho + r * c0, cnt, stride=r)

            s = [lax.dot_general(q_refs[h][rows(q0, tqb), :].astype(BF16),
                                 k_refs[h][rows(k0, wk), :].astype(BF16), _NT,
                                 preferred_element_type=F32) + bias for h in range(hg)]
            m_new = [jnp.broadcast_to(jnp.max(s[h], axis=-1, keepdims=True), (tqb, HEAD_DIM))
                     for h in range(hg)]
            p = [jnp.exp(s[h] - m_new[h][:, :1]) for h in range(hg)]
            l_new = [jnp.broadcast_to(jnp.sum(p[h], axis=-1, keepdims=True), (tqb, HEAD_DIM))
                     for h in range(hg)]
            acc_new = [jnp.dot(p[h].astype(BF16), v_refs[h][rows(k0, wk), :].astype(BF16),
                               preferred_element_type=F32) for h in range(hg)]
            for h in range(hg):
                if bi > 0:
                    m_old = m_refs[h][rows(q0, tqb), :]
                    m_all = jnp.maximum(m_old, m_new[h])
                    w_old = jnp.exp(m_old - m_all)
                    w_new = jnp.exp(m_new[h] - m_all)
                    acc_new[h] = acc_refs[h][rows(q0, tqb), :] * w_old + acc_new[h] * w_new
                    l_new[h] = l_refs[h][rows(q0, tqb), :] * w_old + l_new[h] * w_new
                    m_new[h] = m_all
                acc_refs[h][rows(q0, tqb), :] = acc_new[h]
                l_refs[h][rows(q0, tqb), :] = l_new[h]
                m_refs[h][rows(q0, tqb), :] = m_new[h]
            return carry

        lax.fori_loop(0, r * nqb, block, 0)

    for h in range(hg):
        o_ref[:, h * HEAD_DIM:(h + 1) * HEAD_DIM] = (acc_refs[h][...] / l_refs[h][...]).astype(o_ref.dtype)


def dilated_attention(qkv, *, batch, seq, heads, heads_per_step=2, tq=256):
    t = batch * seq
    hd = HEAD_DIM
    hg = heads_per_step
    branches = tuple((dil, window // (2 * dil)) for window, dil in DIL_BRANCHES)
    steps = branches[0][1]
    tq_max = min(tq, seq)
    specs = lambda part: [pl.BlockSpec((seq, hd), lambda b, g, h=h: (b, part * heads + g * hg + h))
                          for h in range(hg)]
    return pl.pallas_call(
        functools.partial(_dilated_attn_body, hg=hg, branches=branches, tq=tq),
        grid=(batch, heads // hg),
        in_specs=specs(0) + specs(1) + specs(2),
        out_specs=pl.BlockSpec((seq, hg * hd), lambda b, g: (b, g)),
        out_shape=jax.ShapeDtypeStruct((t, heads * hd), BF16),
        scratch_shapes=([pltpu.VMEM((seq, hd), F32)] * (3 * hg)
                        + [pltpu.VMEM((3, tq_max, min(seq, tq_max + 2 * steps)), F32)]),
        compiler_params=_params(("parallel", "parallel"),
                                hg * seq * hd * (2 * 3 * 4 + 2 * 2 + 3 * 4) + (16 << 20)),
        name="dilated_attn",
    )(*([qkv] * (3 * hg)))


def _router_body(x_ref, g_ref, wr_ref, hn_ref, e_ref, p_ref):
    hn = _rms_rows(x_ref[...], g_ref[...])
    hn_ref[...] = hn
    logits = jnp.dot(hn, wr_ref[...], precision=lax.Precision.HIGHEST, preferred_element_type=F32)
    lane = lax.broadcasted_iota(jnp.int32, logits.shape, 1)
    logits = jnp.where(lane < N_EXPERTS, logits, -jnp.inf)
    m1 = jnp.max(logits, axis=-1, keepdims=True)
    i1 = jnp.min(jnp.where(logits == m1, lane, LANES), axis=-1, keepdims=True)
    rest = jnp.where(lane == i1, -jnp.inf, logits)
    m2 = jnp.max(rest, axis=-1, keepdims=True)
    i2 = jnp.min(jnp.where(rest == m2, lane, LANES), axis=-1, keepdims=True)
    e2 = jnp.exp(m2 - m1)
    den = 1.0 + e2
    e_ref[...] = jnp.where(lane == 0, i1, jnp.where(lane == 1, i2, 0))
    p_ref[...] = jnp.where(lane == 0, 1.0 / den, jnp.where(lane == 1, e2 / den, 0.0))


def moe_router(x, gain, w_router_padded, tm=512):
    m, d = x.shape
    return pl.pallas_call(
        _router_body,
        grid=(m // tm,),
        in_specs=[pl.BlockSpec((tm, d), lambda i: (i, 0)),
                  pl.BlockSpec((1, d), lambda i: (0, 0)),
                  pl.BlockSpec((d, LANES), lambda i: (0, 0))],
        out_specs=[pl.BlockSpec((tm, d), lambda i: (i, 0)),
                   pl.BlockSpec((tm, LANES), lambda i: (i, 0)),
                   pl.BlockSpec((tm, LANES), lambda i: (i, 0))],
        out_shape=[jax.ShapeDtypeStruct((m, d), F32),
                   jax.ShapeDtypeStruct((m, LANES), jnp.int32),
                   jax.ShapeDtypeStruct((m, LANES), F32)],
        compiler_params=_params(("parallel",), 6 * tm * d * 4 + (8 << 20)),
        name="moe_router",
    )(x, gain.reshape(1, d), w_router_padded)


def _gather_rows_per_step(tb, nf):
    per_step = -(-tb // nf)
    while (per_step * nf) % 8:
        per_step += 1
    return per_step


def _expert_body(blk_e_ref, n_used_ref, tok_ref, tok_next_ref, hn_hbm, wg_ref, wu_ref, wd_ref,
                 y_ref, xf_ref, xb_ref, sem, *, nf, nblk):
    b = pl.program_id(0)
    f = pl.program_id(1)
    tb = y_ref.shape[0]
    n_used = n_used_ref[0]
    used = b < n_used
    slot = lax.rem(b, 2)
    per_step = _gather_rows_per_step(tb, nf)

    def row_copy(toks, j, dst_slot):
        return pltpu.make_async_copy(hn_hbm.at[pl.ds(toks[0, jnp.minimum(j, tb - 1)], 1), :],
                                     xf_ref.at[dst_slot, pl.ds(j, 1), :], sem.at[dst_slot])

    def wait_gather(dst_slot):
        rows = per_step * nf
        pltpu.make_async_copy(hn_hbm.at[pl.ds(0, rows), :], xf_ref.at[dst_slot], sem.at[dst_slot]).wait()

    @pl.when((b == 0) & (f == 0))
    def _():
        def start(j, c):
            row_copy(tok_ref, j, 0).start()
            return c
        lax.fori_loop(0, per_step * nf, start, 0)

    @pl.when(used & (f == 0))
    def _():
        wait_gather(slot)
        xb_ref[...] = xf_ref[slot, :tb, :].astype(BF16)
        y_ref[...] = jnp.zeros_like(y_ref)

    @pl.when(~used & (f == 0))
    def _():
        y_ref[...] = jnp.zeros_like(y_ref)

    @pl.when(used)
    def _():
        for j in range(per_step):
            row_copy(tok_next_ref, f * per_step + j, 1 - slot).start()
        xb = xb_ref[...]
        a = jnp.dot(xb, wg_ref[...], preferred_element_type=F32)
        u = jnp.dot(xb, wu_ref[...], preferred_element_type=F32)
        h = (_silu(a) * u).astype(BF16)
        y_ref[...] += jnp.dot(h, wd_ref[...], preferred_element_type=F32)

    @pl.when(used & (f == nf - 1) & (b + 1 >= n_used))
    def _():
        wait_gather(1 - slot)


def moe_experts(hn, slot_tok, blk_e, n_used, wg, wu, wd, tb, tf=512):
    t, d = hn.shape
    p = slot_tok.shape[0]
    nblk = p // tb
    fdim = wg.shape[2]
    nf = fdim // tf

    def live(b, n_used_ref):
        return jnp.minimum(b, n_used_ref[0] - 1)

    def wmap_cols(b, f, blk_e_ref, n_used_ref):
        return (blk_e_ref[live(b, n_used_ref)], 0, jnp.where(b < n_used_ref[0], f, nf - 1))

    def wmap_rows(b, f, blk_e_ref, n_used_ref):
        return (blk_e_ref[live(b, n_used_ref)], jnp.where(b < n_used_ref[0], f, nf - 1), 0)

    vmem = (tb * d * (2 * 4 + 2) + 2 * tb * d * 4 + 2 * (2 * d * tf + tf * d) * 2 + 3 * tb * tf * 4
            + tb * d * 4 + (4 << 20))
    toks = slot_tok.reshape(nblk, 1, tb)
    grid_spec = pltpu.PrefetchScalarGridSpec(
        num_scalar_prefetch=2,
        grid=(nblk, nf),
        in_specs=[pl.BlockSpec((None, 1, tb), lambda b, f, *_: (b, 0, 0), memory_space=pltpu.SMEM),
                  pl.BlockSpec((None, 1, tb), lambda b, f, *_: (jnp.minimum(b + 1, nblk - 1), 0, 0),
                               memory_space=pltpu.SMEM),
                  pl.BlockSpec(memory_space=pl.ANY),
                  pl.BlockSpec((None, d, tf), wmap_cols),
                  pl.BlockSpec((None, d, tf), wmap_cols),
                  pl.BlockSpec((None, tf, d), wmap_rows)],
        out_specs=pl.BlockSpec((tb, d), lambda b, f, *_: (b, 0)),
        scratch_shapes=[pltpu.VMEM((2, _gather_rows_per_step(tb, nf) * nf, d), F32), pltpu.VMEM((tb, d), BF16),
                        pltpu.SemaphoreType.DMA((2,))],
    )
    return pl.pallas_call(
        functools.partial(_expert_body, nf=nf, nblk=nblk),
        grid_spec=grid_spec,
        out_shape=jax.ShapeDtypeStruct((p, d), F32),
        compiler_params=_params(("arbitrary", "arbitrary"), vmem),
        name="moe_experts",
    )(blk_e, n_used, toks, toks, hn, wg, wu, wd)


def _combine_body(slot_ref, slot_next_ref, x_ref, p_ref, y_hbm, o_ref, buf_ref, sem):
    i = pl.program_id(0)
    nt = pl.num_programs(0)
    tm = x_ref.shape[0]
    cur = lax.rem(i, 2)

    def row_copy(slots, j, half):
        k, r = divmod(j, tm)
        return pltpu.make_async_copy(y_hbm.at[pl.ds(slots[0, j], 1), :],
                                     buf_ref.at[half, k, pl.ds(r, 1), :], sem.at[half])

    def wait_tile(half):
        for k in range(TOP_K):
            pltpu.make_async_copy(y_hbm.at[pl.ds(0, tm), :], buf_ref.at[half, k], sem.at[half]).wait()

    @pl.when(i == 0)
    def _():
        for j in range(TOP_K * tm):
            row_copy(slot_ref, j, 0).start()

    for j in range(TOP_K * tm):
        row_copy(slot_next_ref, j, 1 - cur).start()
    wait_tile(cur)
    gates = p_ref[...]
    o_ref[...] = x_ref[...] + (gates[:, 0:1] * buf_ref[cur, 0] + gates[:, 1:2] * buf_ref[cur, 1])

    @pl.when(i == nt - 1)
    def _():
        wait_tile(1 - cur)


def moe_combine(x, y, tok_slots, gates, tm=256):
    t, d = x.shape
    nt = t // tm
    slots = tok_slots.reshape(nt, tm, TOP_K).transpose(0, 2, 1).reshape(nt, 1, TOP_K * tm)
    slot_spec = lambda index_map: pl.BlockSpec((None, 1, TOP_K * tm), index_map, memory_space=pltpu.SMEM)
    return pl.pallas_call(
        _combine_body,
        grid=(nt,),
        in_specs=[slot_spec(lambda i: (i, 0, 0)),
                  slot_spec(lambda i: (jnp.minimum(i + 1, nt - 1), 0, 0)),
                  pl.BlockSpec((tm, d), lambda i: (i, 0)),
                  pl.BlockSpec((tm, LANES), lambda i: (i, 0)),
                  pl.BlockSpec(memory_space=pl.ANY)],
        out_specs=pl.BlockSpec((tm, d), lambda i: (i, 0)),
        out_shape=jax.ShapeDtypeStruct((t, d), F32),
        scratch_shapes=[pltpu.VMEM((2, TOP_K, tm, d), F32), pltpu.SemaphoreType.DMA((2,))],
        compiler_params=_params(("arbitrary",), (4 + 2 * TOP_K) * tm * d * 4 + (4 << 20)),
        name="moe_combine",
    )(slots, slots, x, gates, y)


def _routing_plan(top_e, tb):
    t = top_e.shape[0]
    n = t * TOP_K
    e_flat = top_e.reshape(n)
    onehot = (e_flat[:, None] == jnp.arange(N_EXPERTS, dtype=jnp.int32)[None, :]).astype(jnp.int32)
    rank = jnp.take_along_axis(jnp.cumsum(onehot, axis=0) - onehot, e_flat[:, None], axis=1)[:, 0]
    counts = jnp.sum(onehot, axis=0)
    padded = (counts + tb - 1) // tb * tb
    pend = jnp.cumsum(padded)
    dest = (pend - padded)[e_flat] + rank
    nblk = -(-n // tb) + N_EXPERTS
    p = nblk * tb
    tok_flat = jnp.arange(n, dtype=jnp.int32) // TOP_K
    slot_tok = jnp.zeros((p,), jnp.int32).at[dest].set(tok_flat)
    blk_start = jnp.arange(nblk, dtype=pend.dtype) * tb
    blk_e = jnp.minimum(jnp.searchsorted(pend, blk_start, side='right'), N_EXPERTS - 1).astype(jnp.int32)
    n_used = (pend[-1] // tb).astype(jnp.int32).reshape(1)
    return slot_tok, blk_e, n_used, dest.reshape(t, TOP_K).astype(jnp.int32)


def _rope_tables(seq, hd):
    half = hd // 2
    inv_freq = ROPE_THETA ** (-jnp.arange(half, dtype=F32) * 2.0 / hd)
    ang = jnp.arange(seq, dtype=F32)[:, None] * inv_freq[None, :]
    cos, sin = jnp.cos(ang), jnp.sin(ang)
    return jnp.concatenate([cos, cos], axis=-1), jnp.concatenate([-sin, sin], axis=-1)


def kernel(x, norm_gains, hgrn_w_in, hgrn_lb_logits, hgrn_onorm, hgrn_w_out, attn_w_qkv, attn_q_gain,
           attn_k_gain, attn_w_out, ffn_w_gate, ffn_w_up, ffn_w_down, moe_w_router, moe_w_gate, moe_w_up,
           moe_w_down):
    batch, seq, d = x.shape
    t = batch * seq
    heads = d // HEAD_DIM
    hd = HEAD_DIM
    xf = x.reshape(t, d)

    w_in = hgrn_w_in[0]
    w_qvg = jnp.concatenate([w_in[:, :d], w_in[:, 3 * d:5 * d]], axis=1).astype(BF16)
    w_f = w_in[:, d:3 * d].astype(BF16)
    lb = jnp.cumsum(jax.nn.softmax(hgrn_lb_logits.astype(F32), axis=0), axis=0)[0].reshape(1, 2 * d)
    tn = min(1024, d)
    nq = d // tn

    def qvg_epilogue(acc, j, aux, outs, cols):
        is_v = (j >= nq) & (j < 2 * nq)
        outs[0][:, cols] = jnp.where(is_v, acc, _silu(acc)).astype(BF16)

    (qvg,) = norm_matmul(
        "hgrn_qvg_proj", xf, norm_gains[0, 0], w_qvg, qvg_epilogue,
        [jax.ShapeDtypeStruct((t, 3 * d), BF16)], [pl.BlockSpec((1024, tn), lambda i, j: (i, j))], tn=tn)

    def f_epilogue(acc, j, aux, outs, cols):
        lbv = aux[0][:, cols]
        fgate = lbv + (1.0 - lbv) * jax.nn.sigmoid(acc)
        outs[0][:, cols] = jnp.log2(fgate)
        outs[1][:, cols] = (1.0 - fgate).astype(BF16)

    log2f, kk = norm_matmul(
        "hgrn_forget_proj", xf, norm_gains[0, 0], w_f, f_epilogue,
        [jax.ShapeDtypeStruct((t, 2 * d), F32), jax.ShapeDtypeStruct((t, 2 * d), BF16)],
        [pl.BlockSpec((1024, tn), lambda i, j: (i, j)), pl.BlockSpec((1024, tn), lambda i, j: (i, j))],
        aux=(lb,), aux_specs=(pl.BlockSpec((1, tn), lambda i, j: (0, j)),), tn=tn)

    n_exp, _, f_exp = moe_w_gate.shape[1:]
    later_f32 = (moe_w_gate[0].reshape(n_exp * d, f_exp), moe_w_up[0].reshape(n_exp * d, f_exp),
                 moe_w_down[0].reshape(n_exp * f_exp, d), attn_w_qkv[0], attn_w_out[0])
    o_f, o_b, (moe_wg, moe_wu, moe_wd, w_qkv, w_attn_out) = gla_bidirectional(
        qvg, kk, log2f, batch=batch, seq=seq, heads=heads, cast=later_f32)

    tm = 512
    x1 = proj_residual(
        "hgrn_out_proj", (o_f, o_b, qvg, hgrn_onorm[0].reshape(1, d)),
        (pl.BlockSpec((tm, d), lambda i, j: (i, 0)), pl.BlockSpec((tm, d), lambda i, j: (i, 0)),
         pl.BlockSpec((tm, d), lambda i, j: (i, 2)), pl.BlockSpec((1, d), lambda i, j: (0, 0))),
        _hgrn_out_prologue, hgrn_w_out[0].astype(BF16), xf, tm=tm)

    fdim = ffn_w_gate.shape[2]
    fpad = -(-fdim // 512) * 512 - fdim
    wg = jnp.pad(ffn_w_gate[0], ((0, 0), (0, fpad))).astype(BF16)
    wu = jnp.pad(ffn_w_up[0], ((0, 0), (0, fpad))).astype(BF16)
    wd = jnp.pad(ffn_w_down[0], ((0, fpad), (0, 0))).astype(BF16)
    x2 = norm_swiglu_residual(x1, norm_gains[0, 1], wg, wu, wd)

    cos, sin = _rope_tables(seq, hd)
    qg = attn_q_gain[0].reshape(1, hd)
    kg = attn_k_gain[0].reshape(1, hd)
    tm_qkv = 1024
    tn_qkv = 512
    nq_qkv = d // tn_qkv
    pos_blocks = seq // tm_qkv

    def qkv_epilogue(acc, j, aux, outs, cols):
        cos_ref, sin_ref, qg_ref, kg_ref = aux
        o_ref = outs[0]

        def normed_rope(gain, scale):
            for h in range(acc.shape[1] // hd):
                y = _rms_rows(acc[:, h * hd:(h + 1) * hd], gain)
                y = y * cos_ref[...] + pltpu.roll(y, hd // 2, 1) * sin_ref[...]
                o_ref[:, cols.start + h * hd:cols.start + (h + 1) * hd] = y * scale

        @pl.when(j < nq_qkv)
        def _():
            normed_rope(qg_ref[...], hd ** -0.5)

        @pl.when((j >= nq_qkv) & (j < 2 * nq_qkv))
        def _():
            normed_rope(kg_ref[...], 1.0)

        @pl.when(j >= 2 * nq_qkv)
        def _():
            o_ref[:, cols] = acc

    (qkv,) = norm_matmul(
        "attn_qkv_proj", x2, norm_gains[1, 0], w_qkv, qkv_epilogue,
        [jax.ShapeDtypeStruct((t, 3 * d), F32)], [pl.BlockSpec((tm_qkv, tn_qkv), lambda i, j: (i, j))],
        aux=(cos, sin, qg, kg),
        aux_specs=(pl.BlockSpec((tm_qkv, hd), lambda i, j: (i % pos_blocks, 0)),
                   pl.BlockSpec((tm_qkv, hd), lambda i, j: (i % pos_blocks, 0)),
                   pl.BlockSpec((1, hd), lambda i, j: (0, 0)),
                   pl.BlockSpec((1, hd), lambda i, j: (0, 0))),
        tm=tm_qkv, tn=tn_qkv, sub=tn_qkv)

    attn = dilated_attention(qkv, batch=batch, seq=seq, heads=heads)
    x3 = proj_residual("attn_out_proj", (attn,), (pl.BlockSpec((tm, d), lambda i, j: (i, 0)),),
                       _copy_prologue, w_attn_out, x2, tm=tm)

    tb = 512
    w_router = jnp.pad(moe_w_router[0].astype(F32), ((0, 0), (0, LANES - N_EXPERTS)))
    hn3, e_pad, p_pad = moe_router(x3, norm_gains[1, 1], w_router)
    slot_tok, blk_e, n_used, tok_slots = _routing_plan(e_pad[:, :TOP_K], tb)
    y = moe_experts(hn3, slot_tok, blk_e, n_used, moe_wg.reshape(n_exp, d, f_exp),
                    moe_wu.reshape(n_exp, d, f_exp), moe_wd.reshape(n_exp, f_exp, d), tb)
    out = moe_combine(x3, y, tok_slots, p_pad)
    return out.reshape(batch, seq, d)
```

```python
import functools

import jax
import jax.numpy as jnp
from jax import lax
from jax.experimental import pallas as pl
from jax.experimental.pallas import tpu as pltpu

F32 = jnp.float32
BF16 = jnp.bfloat16
EPS = 1e-6
NEG_INF = -1e30
ROPE_THETA = 10000.0

HEAD_DIM = 128
GLA_CHUNK = 64
DIL_BRANCHES = ((128, 1), (512, 4), (2048, 16))
N_EXPERTS = 8
TOP_K = 2

LANES = 128
V7X_VMEM_BYTES = 64 * 1024 * 1024
VMEM_BUDGET = 56 * 1024 * 1024

_NT = (((1,), (1,)), ((), ()))
_TN = (((0,), (0,)), ((), ()))


def _params(semantics, vmem_bytes):
    return pltpu.CompilerParams(dimension_semantics=semantics,
                                vmem_limit_bytes=int(min(vmem_bytes, VMEM_BUDGET)))


def _silu(x):
    return x * jax.nn.sigmoid(x)


def _rms_rows(x, gain):
    ms = jnp.mean(x * x, axis=-1, keepdims=True)
    return x * lax.rsqrt(ms + EPS) * gain


def _norm_matmul_body(x_ref, g_ref, w_ref, *rest, n_aux, epilogue, sub):
    aux, outs, hn_ref = rest[:n_aux], rest[n_aux:-1], rest[-1]
    j = pl.program_id(1)

    @pl.when(j == 0)
    def _():
        hn_ref[...] = _rms_rows(x_ref[...], g_ref[...]).astype(BF16)

    for s in range(w_ref.shape[1] // sub):
        cols = slice(s * sub, (s + 1) * sub)
        acc = jnp.dot(hn_ref[...], w_ref[:, cols], preferred_element_type=F32)
        epilogue(acc, j, aux, outs, cols)


def norm_matmul(name, x, gain, w, epilogue, out_shapes, out_specs, aux=(), aux_specs=(), tm=1024, tn=512,
                sub=256):
    m, d = x.shape
    n = w.shape[1]
    assert m % tm == 0 and n % tn == 0
    out_bytes = sum(2 * tm * tn * jnp.dtype(s.dtype).itemsize for s in out_shapes)
    vmem = 2 * tm * d * 4 + tm * d * 2 + 2 * d * tn * 2 + out_bytes + 4 * tm * tn * 4 + (4 << 20)
    return pl.pallas_call(
        functools.partial(_norm_matmul_body, n_aux=len(aux), epilogue=epilogue, sub=min(sub, tn)),
        grid=(m // tm, n // tn),
        in_specs=[pl.BlockSpec((tm, d), lambda i, j: (i, 0)),
                  pl.BlockSpec((1, d), lambda i, j: (0, 0)),
                  pl.BlockSpec((d, tn), lambda i, j: (0, j)),
                  *aux_specs],
        out_specs=out_specs,
        out_shape=out_shapes,
        scratch_shapes=[pltpu.VMEM((tm, d), BF16)],
        compiler_params=_params(("parallel", "arbitrary"), vmem),
        name=name,
    )(x, gain.reshape(1, d), w, *aux)


def _gla_chunks(chains):
    c = chains[0]["q"].shape[0]
    n_levels = c.bit_length()
    for ch in chains:
        hi = ch["lf2"].astype(BF16)
        lo = (ch["lf2"] - hi.astype(F32)).astype(BF16)
        ch["s"] = jnp.dot(ch["sums"], jnp.concatenate([hi, lo], axis=0), preferred_element_type=F32)
    for ch in chains:
        qd = ch["q"] * jnp.exp2(ch["s"][:c]).astype(BF16)
        ch["o"] = lax.dot_general(qd, ch["st"].astype(BF16), _NT, preferred_element_type=F32)
        ch["a"] = jnp.zeros((c, 2 * c), F32)
    for level in range(0, n_levels, 2):
        for ch in chains:
            qs, ks = [], []
            for l in (level, level + 1):
                if l == 0:
                    qs.append(ch["q"])
                    ks.append(ch["k"])
                elif l < n_levels:
                    e = jnp.exp2(-jnp.abs(ch["s"][l * c:(l + 1) * c])).astype(BF16)
                    qs.append(ch["q"] * e)
                    ks.append(ch["k"] * e)
            if len(ks) == 1:
                ks = ks * 2
            p = lax.dot_general(jnp.concatenate(qs, axis=0), jnp.concatenate(ks, axis=0), _NT,
                                preferred_element_type=F32)
            for i in range(len(qs)):
                ch["a"] = jnp.where(ch["lvl"] == 2 * (level + i) + i, p[i * c:(i + 1) * c, :], ch["a"])
    outs = []
    for ch in chains:
        g = ch["s"][:c]
        g_tot = g[c - 1:c, :] if ch["fwd"] else g[0:1, :]
        vv = jnp.concatenate([ch["v"], ch["v"]], axis=0)
        o = ch["o"] + jnp.dot(ch["a"].astype(BF16), vv, preferred_element_type=F32)
        kd = ch["k"] * jnp.exp2(g_tot - g).astype(BF16)
        st_new = (ch["st"] * jnp.exp2(g_tot)
                  + lax.dot_general(ch["v"], kd, _TN, preferred_element_type=F32))
        outs.append((o, st_new))
    return outs


def _gla_tables(c, fwd):
    ti = lax.broadcasted_iota(jnp.int32, (c, c), 0)
    ui = lax.broadcasted_iota(jnp.int32, (c, c), 1)

    def cum(row):
        return ((ui <= row) if fwd else (ui >= row)).astype(jnp.int32)

    blocks = [cum(ti)]
    half = 1
    while half < c:
        boundary = (ti & -(2 * half)) + (half - 1 if fwd else half)
        blocks.append(cum(ti) - cum(boundary))
        half *= 2
    x = ti ^ ui
    top_bit = sum((x >= (1 << b)).astype(jnp.int32) for b in range(1, c.bit_length() - 1))
    lvl = jnp.where(ti == ui, 0, jnp.where((ui < ti) if fwd else (ui > ti), 1 + top_bit, -1))
    sums = jnp.concatenate(blocks, axis=0).astype(F32).astype(BF16)
    codes = jnp.concatenate([2 * lvl, 2 * lvl + 1], axis=1)
    return jnp.concatenate([sums, sums], axis=1), codes


def _gla_body(qf_ref, kf_ref, vf_ref, lf_ref, qb_ref, kb_ref, vb_ref, lb_ref, *rest, chunk, n_cast):
    cast_in, (of_ref, ob_ref) = rest[:n_cast], rest[n_cast:n_cast + 2]
    cast_out, (st_ref, sums_ref, lvl_ref) = rest[n_cast + 2:2 * n_cast + 2], rest[2 * n_cast + 2:]
    rows, width = qf_ref.shape
    nch = rows // chunk
    c = chunk

    for src, dst in zip(cast_in, cast_out):
        dst[...] = src[...].astype(dst.dtype)

    @pl.when(pl.program_id(2) == 0)
    def _():
        st_ref[...] = jnp.zeros_like(st_ref)

    dirs = []
    for d, (fwd, refs) in enumerate(((True, (qf_ref, kf_ref, vf_ref, lf_ref, of_ref)),
                                     (False, (qb_ref, kb_ref, vb_ref, lb_ref, ob_ref)))):
        sums, lvl = _gla_tables(c, fwd)
        sums_ref[d] = sums
        lvl_ref[d] = lvl
        dirs.append((fwd, refs))

    def one_chunk(ci, carry):
        chains, dests = [], []
        for d, (fwd, (q_ref, k_ref, v_ref, l_ref, o_ref)) in enumerate(dirs):
            r0 = pl.multiple_of((ci if fwd else nch - 1 - ci) * c, c)
            for h in range(width // HEAD_DIM):
                hs = slice(h * HEAD_DIM, (h + 1) * HEAD_DIM)
                chains.append(dict(q=q_ref[pl.ds(r0, c), hs], k=k_ref[pl.ds(r0, c), hs],
                                   v=v_ref[pl.ds(r0, c), hs], lf2=l_ref[pl.ds(r0, c), hs],
                                   st=st_ref[d, h], lvl=lvl_ref[d], fwd=fwd, sums=sums_ref[d]))
                dests.append((o_ref, r0, hs, d, h))
        for (o, st_new), (o_ref, r0, hs, d, h) in zip(_gla_chunks(chains), dests):
            o_ref[pl.ds(r0, c), hs] = o.astype(o_ref.dtype)
            st_ref[d, h] = st_new
        return carry

    lax.fori_loop(0, nch, one_chunk, 0)


def _cast_block_rows(rows, steps):
    br = 16
    while rows % br or rows // br > steps:
        br += 16
    return br


def gla_bidirectional(qvg, kk, log2f, *, batch, seq, heads, cast=(), heads_per_step=4, rows_per_step=512):
    t = batch * seq
    hg = heads_per_step
    width = hg * HEAD_DIM
    rb = rows_per_step
    ns = seq // rb
    ng = heads // hg

    def spec(fwd, seg):
        return pl.BlockSpec((rb, width),
                            lambda b, g, s: (b * ns + (s if fwd else ns - 1 - s), seg * ng + g))

    cast_specs = []
    for arr in cast:
        br = _cast_block_rows(arr.shape[0], batch * ng * ns)
        last = arr.shape[0] // br - 1
        cast_specs.append(pl.BlockSpec(
            (br, arr.shape[1]), lambda b, g, s, last=last: (jnp.minimum((b * ng + g) * ns + s, last), 0)))
    cast_bytes = sum(2 * s.block_shape[0] * s.block_shape[1] * (4 + 2) for s in cast_specs)
    outs = pl.pallas_call(
        functools.partial(_gla_body, chunk=GLA_CHUNK, n_cast=len(cast)),
        grid=(batch, ng, ns),
        in_specs=[spec(True, 0), spec(True, 0), spec(True, 1), spec(True, 0),
                  spec(False, 0), spec(False, 1), spec(False, 1), spec(False, 1), *cast_specs],
        out_specs=[spec(True, 0), spec(False, 0), *cast_specs],
        out_shape=[jax.ShapeDtypeStruct((t, heads * HEAD_DIM), BF16)] * 2
                  + [jax.ShapeDtypeStruct(arr.shape, BF16) for arr in cast],
        scratch_shapes=[pltpu.VMEM((2, hg, HEAD_DIM, HEAD_DIM), F32),
                        pltpu.VMEM((2, GLA_CHUNK * GLA_CHUNK.bit_length(), 2 * GLA_CHUNK), BF16),
                        pltpu.VMEM((2, GLA_CHUNK, 2 * GLA_CHUNK), jnp.int32)],
        compiler_params=_params(("arbitrary", "arbitrary", "arbitrary"), cast_bytes + (24 << 20)),
        name="gla_bidir",
    )(qvg, kk, qvg, log2f, qvg, kk, qvg, log2f, *cast)
    return outs[0], outs[1], outs[2:]


def _proj_residual_body(*refs, n_pro, prologue):
    pro, (w_ref, x_ref, o_ref, y_ref) = refs[:n_pro], refs[n_pro:]

    @pl.when(pl.program_id(1) == 0)
    def _():
        prologue(pro, y_ref)

    o_ref[...] = x_ref[...] + jnp.dot(y_ref[...], w_ref[...], preferred_element_type=F32)


def proj_residual(name, pro_inputs, pro_specs, prologue, w, xres, tm=512, tn=1024):
    m, n = xres.shape
    tn = min(tn, n)
    kdim = w.shape[0]
    pro_bytes = sum(2 * tm * kdim * jnp.dtype(a.dtype).itemsize for a in pro_inputs)
    vmem = pro_bytes + tm * kdim * 2 + 2 * kdim * tn * 2 + 4 * tm * tn * 4 + 4 * tm * kdim * 4 + (4 << 20)
    return pl.pallas_call(
        functools.partial(_proj_residual_body, n_pro=len(pro_inputs), prologue=prologue),
        grid=(m // tm, n // tn),
        in_specs=[*pro_specs,
                  pl.BlockSpec((kdim, tn), lambda i, j: (0, j)),
                  pl.BlockSpec((tm, tn), lambda i, j: (i, j))],
        out_specs=pl.BlockSpec((tm, tn), lambda i, j: (i, j)),
        out_shape=jax.ShapeDtypeStruct((m, n), F32),
        scratch_shapes=[pltpu.VMEM((tm, kdim), BF16)],
        compiler_params=_params(("parallel", "arbitrary"), vmem),
        name=name,
    )(*pro_inputs, w, xres)


def _hgrn_out_prologue(pro, y_ref):
    of_ref, ob_ref, gate_ref, gain_ref = pro
    for h in range(of_ref.shape[1] // HEAD_DIM):
        hs = slice(h * HEAD_DIM, (h + 1) * HEAD_DIM)
        o = of_ref[:, hs].astype(F32) + ob_ref[:, hs].astype(F32)
        y = _rms_rows(o, gain_ref[:, hs]) * gate_ref[:, hs].astype(F32)
        y_ref[:, hs] = y.astype(BF16)


def _copy_prologue(pro, y_ref):
    y_ref[...] = pro[0][...]


def _swiglu_body(x_ref, g_ref, wg_ref, wu_ref, wd_ref, o_ref, hn_ref):
    @pl.when(pl.program_id(1) == 0)
    def _():
        x = x_ref[...]
        hn_ref[...] = _rms_rows(x, g_ref[...]).astype(BF16)
        o_ref[...] = x

    hn = hn_ref[...]
    a = jnp.dot(hn, wg_ref[...], preferred_element_type=F32)
    u = jnp.dot(hn, wu_ref[...], preferred_element_type=F32)
    h = (_silu(a) * u).astype(BF16)
    o_ref[...] += jnp.dot(h, wd_ref[...], preferred_element_type=F32)


def norm_swiglu_residual(x, gain, wg, wu, wd, tm=512, tf=512):
    m, d = x.shape
    f = wg.shape[1]
    assert m % tm == 0 and f % tf == 0
    vmem = 4 * tm * d * 4 + tm * d * 2 + 2 * (2 * d * tf + tf * d) * 2 + 3 * tm * tf * 4 + 2 * tm * d * 4 + (4 << 20)
    return pl.pallas_call(
        _swiglu_body,
        grid=(m // tm, f // tf),
        in_specs=[pl.BlockSpec((tm, d), lambda i, j: (i, 0)),
                  pl.BlockSpec((1, d), lambda i, j: (0, 0)),
                  pl.BlockSpec((d, tf), lambda i, j: (0, j)),
                  pl.BlockSpec((d, tf), lambda i, j: (0, j)),
                  pl.BlockSpec((tf, d), lambda i, j: (j, 0))],
        out_specs=pl.BlockSpec((tm, d), lambda i, j: (i, 0)),
        out_shape=jax.ShapeDtypeStruct((m, d), F32),
        scratch_shapes=[pltpu.VMEM((tm, d), BF16)],
        compiler_params=_params(("parallel", "arbitrary"), vmem),
        name="dense_swiglu",
    )(x, gain.reshape(1, d), wg, wu, wd)


def _dilated_attn_body(*refs, hg, branches, tq):
    q_refs, k_refs, v_refs = (refs[i * hg:(i + 1) * hg] for i in range(3))
    o_ref = refs[3 * hg]
    acc_refs, m_refs, l_refs = (refs[3 * hg + 1 + i * hg:3 * hg + 1 + (i + 1) * hg] for i in range(3))
    bias_ref = refs[6 * hg + 1]
    seq = q_refs[0].shape[0]
    steps = branches[0][1]
    tq_max = min(tq, seq)
    wk_max = min(seq, tq_max + 2 * steps)

    d = (lax.broadcasted_iota(jnp.int32, (tq_max, wk_max), 1)
         - lax.broadcasted_iota(jnp.int32, (tq_max, wk_max), 0))
    for shift in range(3):
        bias_ref[shift] = jnp.where(jnp.abs(d - shift * steps) <= steps, 0.0, NEG_INF)

    for bi, (r, br_steps) in enumerate(branches):
        assert br_steps == steps
        n = seq // r
        tqb = min(tq, n)
        wk = min(n, tqb + 2 * steps)
        nqb = n // tqb

        def block(idx, carry, bi=bi, r=r, n=n, tqb=tqb, wk=wk, nqb=nqb):
            rho = idx // nqb
            q0 = (idx - rho * nqb) * tqb
            k0 = jnp.clip(q0 - steps, 0, n - wk)
            bias = bias_ref[(q0 - k0) // steps, :tqb, :wk]

            def rows(c0, cnt):
                if r == 1:
                    return pl.ds(pl.multiple_of(c0, steps), cnt)
                return pl.ds(rho + r * c0, cnt, stride=r)

            s = [lax.dot_general(q_refs[h][rows(q0, tqb), :].astype(BF16),
                                 k_refs[h][rows(k0, wk), :].astype(BF16), _NT,
                                 preferred_element_type=F32) + bias for h in range(hg)]
            m_new = [jnp.broadcast_to(jnp.max(s[h], axis=-1, keepdims=True), (tqb, HEAD_DIM))
                     for h in range(hg)]
            p = [jnp.exp(s[h] - m_new[h][:, :1]) for h in range(hg)]
            l_new = [jnp.broadcast_to(jnp.sum(p[h], axis=-1, keepdims=True), (tqb, HEAD_DIM))
                     for h in range(hg)]
            acc_new = [jnp.dot(p[h].astype(BF16), v_refs[h][rows(k0, wk), :].astype(BF16),
                               preferred_element_type=F32) for h in range(hg)]
            for h in range(hg):
                if bi > 0:
                    m_old = m_refs[h][rows(q0, tqb), :]
                    m_all = jnp.maximum(m_old, m_new[h])
                    w_old = jnp.exp(m_old - m_all)
                    w_new = jnp.exp(m_new[h] - m_all)
                    acc_new[h] = acc_refs[h][rows(q0, tqb), :] * w_old + acc_new[h] * w_new
                    l_new[h] = l_refs[h][rows(q0, tqb), :] * w_old + l_new[h] * w_new
                    m_new[h] = m_all
                acc_refs[h][rows(q0, tqb), :] = acc_new[h]
                l_refs[h][rows(q0, tqb), :] = l_new[h]
                m_refs[h][rows(q0, tqb), :] = m_new[h]
            return carry

        lax.fori_loop(0, r * nqb, block, 0)

    for h in range(hg):
        o_ref[:, h * HEAD_DIM:(h + 1) * HEAD_DIM] = (acc_refs[h][...] / l_refs[h][...]).astype(o_ref.dtype)


def dilated_attention(qkv, *, batch, seq, heads, heads_per_step=2, tq=256):
    t = batch * seq
    hd = HEAD_DIM
    hg = heads_per_step
    branches = tuple((dil, window // (2 * dil)) for window, dil in DIL_BRANCHES)
    steps = branches[0][1]
    tq_max = min(tq, seq)
    specs = lambda part: [pl.BlockSpec((seq, hd), lambda b, g, h=h: (b, part * heads + g * hg + h))
                          for h in range(hg)]
    return pl.pallas_call(
        functools.partial(_dilated_attn_body, hg=hg, branches=branches, tq=tq),
        grid=(batch, heads // hg),
        in_specs=specs(0) + specs(1) + specs(2),
        out_specs=pl.BlockSpec((seq, hg * hd), lambda b, g: (b, g)),
        out_shape=jax.ShapeDtypeStruct((t, heads * hd), BF16),
        scratch_shapes=([pltpu.VMEM((seq, hd), F32)] * (3 * hg)
                        + [pltpu.VMEM((3, tq_max, min(seq, tq_max + 2 * steps)), F32)]),
        compiler_params=_params(("parallel", "parallel"),
                                hg * seq * hd * (2 * 3 * 4 + 2 * 2 + 3 * 4) + (16 << 20)),
        name="dilated_attn",
    )(*([qkv] * (3 * hg)))


def _router_body(x_ref, g_ref, wr_ref, hn_ref, e_ref, p_ref):
    hn = _rms_rows(x_ref[...], g_ref[...])
    hn_ref[...] = hn
    logits = jnp.dot(hn, wr_ref[...], precision=lax.Precision.HIGHEST, preferred_element_type=F32)
    lane = lax.broadcasted_iota(jnp.int32, logits.shape, 1)
    logits = jnp.where(lane < N_EXPERTS, logits, -jnp.inf)
    m1 = jnp.max(logits, axis=-1, keepdims=True)
    i1 = jnp.min(jnp.where(logits == m1, lane, LANES), axis=-1, keepdims=True)
    rest = jnp.where(lane == i1, -jnp.inf, logits)
    m2 = jnp.max(rest, axis=-1, keepdims=True)
    i2 = jnp.min(jnp.where(rest == m2, lane, LANES), axis=-1, keepdims=True)
    e2 = jnp.exp(m2 - m1)
    den = 1.0 + e2
    e_ref[...] = jnp.where(lane == 0, i1, jnp.where(lane == 1, i2, 0))
    p_ref[...] = jnp.where(lane == 0, 1.0 / den, jnp.where(lane == 1, e2 / den, 0.0))


def moe_router(x, gain, w_router_padded, tm=512):
    m, d = x.shape
    return pl.pallas_call(
        _router_body,
        grid=(m // tm,),
        in_specs=[pl.BlockSpec((tm, d), lambda i: (i, 0)),
                  pl.BlockSpec((1, d), lambda i: (0, 0)),
                  pl.BlockSpec((d, LANES), lambda i: (0, 0))],
        out_specs=[pl.BlockSpec((tm, d), lambda i: (i, 0)),
                   pl.BlockSpec((tm, LANES), lambda i: (i, 0)),
                   pl.BlockSpec((tm, LANES), lambda i: (i, 0))],
        out_shape=[jax.ShapeDtypeStruct((m, d), F32),
                   jax.ShapeDtypeStruct((m, LANES), jnp.int32),
                   jax.ShapeDtypeStruct((m, LANES), F32)],
        compiler_params=_params(("parallel",), 6 * tm * d * 4 + (8 << 20)),
        name="moe_router",
    )(x, gain.reshape(1, d), w_router_padded)


def _gather_rows_per_step(tb, nf):
    per_step = -(-tb // nf)
    while (per_step * nf) % 8:
        per_step += 1
    return per_step


def _expert_body(blk_e_ref, n_used_ref, blk_rows_ref, tok_ref, tok_next_ref, hn_hbm, wg_ref, wu_ref, wd_ref,
                 y_ref, xf_ref, xb_ref, sem, *, nf, nblk):
    b = pl.program_id(0)
    f = pl.program_id(1)
    tb = y_ref.shape[0]
    n_used = n_used_ref[0]
    used = b < n_used
    slot = lax.rem(b, 2)
    per_step = _gather_rows_per_step(tb, nf)

    def row_copy(toks, j, dst_slot):
        return pltpu.make_async_copy(hn_hbm.at[pl.ds(toks[0, jnp.minimum(j, tb - 1)], 1), :],
                                     xf_ref.at[dst_slot, pl.ds(j, 1), :], sem.at[dst_slot])

    def wait_gather(dst_slot):
        rows = per_step * nf
        pltpu.make_async_copy(hn_hbm.at[pl.ds(0, rows), :], xf_ref.at[dst_slot], sem.at[dst_slot]).wait()

    @pl.when((b == 0) & (f == 0))
    def _():
        def start(j, c):
            row_copy(tok_ref, j, 0).start()
            return c
        lax.fori_loop(0, per_step * nf, start, 0)

    @pl.when(used & (f == 0))
    def _():
        wait_gather(slot)
        xb_ref[...] = xf_ref[slot, :tb, :].astype(BF16)
        y_ref[...] = jnp.zeros_like(y_ref)

    @pl.when(~used & (f == 0))
    def _():
        y_ref[...] = jnp.zeros_like(y_ref)

    def compute(rows):
        for j in range(per_step):
            row_copy(tok_next_ref, f * per_step + j, 1 - slot).start()
        xb = xb_ref[:rows, :]
        a = jnp.dot(xb, wg_ref[...], preferred_element_type=F32)
        u = jnp.dot(xb, wu_ref[...], preferred_element_type=F32)
        h = (_silu(a) * u).astype(BF16)
        y_ref[:rows, :] += jnp.dot(h, wd_ref[...], preferred_element_type=F32)

    live = blk_rows_ref[b]

    @pl.when(used & (live > tb // 2))
    def _():
        compute(tb)

    @pl.when(used & (live <= tb // 2))
    def _():
        compute(tb // 2)

    @pl.when(used & (f == nf - 1) & (b + 1 >= n_used))
    def _():
        wait_gather(1 - slot)


def moe_experts(hn, slot_tok, blk_e, n_used, blk_rows, wg, wu, wd, tb, tf=512):
    t, d = hn.shape
    p = slot_tok.shape[0]
    nblk = p // tb
    fdim = wg.shape[2]
    nf = fdim // tf

    def live(b, n_used_ref):
        return jnp.minimum(b, n_used_ref[0] - 1)

    def wmap_cols(b, f, blk_e_ref, n_used_ref, blk_rows_ref):
        return (blk_e_ref[live(b, n_used_ref)], 0, jnp.where(b < n_used_ref[0], f, nf - 1))

    def wmap_rows(b, f, blk_e_ref, n_used_ref, blk_rows_ref):
        return (blk_e_ref[live(b, n_used_ref)], jnp.where(b < n_used_ref[0], f, nf - 1), 0)

    vmem = (tb * d * (2 * 4 + 2) + 2 * tb * d * 4 + 2 * (2 * d * tf + tf * d) * 2 + 3 * tb * tf * 4
            + tb * d * 4 + (4 << 20))
    toks = slot_tok.reshape(nblk, 1, tb)
    grid_spec = pltpu.PrefetchScalarGridSpec(
        num_scalar_prefetch=3,
        grid=(nblk, nf),
        in_specs=[pl.BlockSpec((None, 1, tb), lambda b, f, *_: (b, 0, 0), memory_space=pltpu.SMEM),
                  pl.BlockSpec((None, 1, tb), lambda b, f, *_: (jnp.minimum(b + 1, nblk - 1), 0, 0),
                               memory_space=pltpu.SMEM),
                  pl.BlockSpec(memory_space=pl.ANY),
                  pl.BlockSpec((None, d, tf), wmap_cols),
                  pl.BlockSpec((None, d, tf), wmap_cols),
                  pl.BlockSpec((None, tf, d), wmap_rows)],
        out_specs=pl.BlockSpec((tb, d), lambda b, f, *_: (b, 0)),
        scratch_shapes=[pltpu.VMEM((2, _gather_rows_per_step(tb, nf) * nf, d), F32), pltpu.VMEM((tb, d), BF16),
                        pltpu.SemaphoreType.DMA((2,))],
    )
    return pl.pallas_call(
        functools.partial(_expert_body, nf=nf, nblk=nblk),
        grid_spec=grid_spec,
        out_shape=jax.ShapeDtypeStruct((p, d), F32),
        compiler_params=_params(("arbitrary", "arbitrary"), vmem),
        name="moe_experts",
    )(blk_e, n_used, blk_rows, toks, toks, hn, wg, wu, wd)


def _combine_body(slot_ref, slot_next_ref, x_ref, p_ref, y_hbm, o_ref, buf_ref, sem):
    i = pl.program_id(0)
    nt = pl.num_programs(0)
    tm = x_ref.shape[0]
    cur = lax.rem(i, 2)

    def row_copy(slots, j, half):
        k, r = divmod(j, tm)
        return pltpu.make_async_copy(y_hbm.at[pl.ds(slots[0, j], 1), :],
                                     buf_ref.at[half, k, pl.ds(r, 1), :], sem.at[half])

    def wait_tile(half):
        for k in range(TOP_K):
            pltpu.make_async_copy(y_hbm.at[pl.ds(0, tm), :], buf_ref.at[half, k], sem.at[half]).wait()

    @pl.when(i == 0)
    def _():
        for j in range(TOP_K * tm):
            row_copy(slot_ref, j, 0).start()

    for j in range(TOP_K * tm):
        row_copy(slot_next_ref, j, 1 - cur).start()
    wait_tile(cur)
    gates = p_ref[...]
    o_ref[...] = x_ref[...] + (gates[:, 0:1] * buf_ref[cur, 0] + gates[:, 1:2] * buf_ref[cur, 1])

    @pl.when(i == nt - 1)
    def _():
        wait_tile(1 - cur)


def moe_combine(x, y, tok_slots, gates, tm=256):
    t, d = x.shape
    nt = t // tm
    slots = tok_slots.reshape(nt, tm, TOP_K).transpose(0, 2, 1).reshape(nt, 1, TOP_K * tm)
    slot_spec = lambda index_map: pl.BlockSpec((None, 1, TOP_K * tm), index_map, memory_space=pltpu.SMEM)
    return pl.pallas_call(
        _combine_body,
        grid=(nt,),
        in_specs=[slot_spec(lambda i: (i, 0, 0)),
                  slot_spec(lambda i: (jnp.minimum(i + 1, nt - 1), 0, 0)),
                  pl.BlockSpec((tm, d), lambda i: (i, 0)),
                  pl.BlockSpec((tm, LANES), lambda i: (i, 0)),
                  pl.BlockSpec(memory_space=pl.ANY)],
        out_specs=pl.BlockSpec((tm, d), lambda i: (i, 0)),
        out_shape=jax.ShapeDtypeStruct((t, d), F32),
        scratch_shapes=[pltpu.VMEM((2, TOP_K, tm, d), F32), pltpu.SemaphoreType.DMA((2,))],
        compiler_params=_params(("arbitrary",), (4 + 2 * TOP_K) * tm * d * 4 + (4 << 20)),
        name="moe_combine",
    )(slots, slots, x, gates, y)


def _routing_plan(top_e, tb):
    t = top_e.shape[0]
    n = t * TOP_K
    e_flat = top_e.reshape(n)
    onehot = (e_flat[:, None] == jnp.arange(N_EXPERTS, dtype=jnp.int32)[None, :]).astype(jnp.int32)
    rank = jnp.take_along_axis(jnp.cumsum(onehot, axis=0) - onehot, e_flat[:, None], axis=1)[:, 0]
    counts = jnp.sum(onehot, axis=0)
    padded = (counts + tb - 1) // tb * tb
    pend = jnp.cumsum(padded)
    dest = (pend - padded)[e_flat] + rank
    nblk = -(-n // tb) + N_EXPERTS
    p = nblk * tb
    tok_flat = jnp.arange(n, dtype=jnp.int32) // TOP_K
    slot_tok = jnp.zeros((p,), jnp.int32).at[dest].set(tok_flat)
    blk_start = jnp.arange(nblk, dtype=pend.dtype) * tb
    blk_e = jnp.minimum(jnp.searchsorted(pend, blk_start, side='right'), N_EXPERTS - 1).astype(jnp.int32)
    n_used = (pend[-1] // tb).astype(jnp.int32).reshape(1)
    blk_rows = jnp.clip(counts[blk_e] - (blk_start - (pend - padded)[blk_e]), 0, tb).astype(jnp.int32)
    return slot_tok, blk_e, n_used, blk_rows, dest.reshape(t, TOP_K).astype(jnp.int32)


def _rope_tables(seq, hd):
    half = hd // 2
    inv_freq = ROPE_THETA ** (-jnp.arange(half, dtype=F32) * 2.0 / hd)
    ang = jnp.arange(seq, dtype=F32)[:, None] * inv_freq[None, :]
    cos, sin = jnp.cos(ang), jnp.sin(ang)
    return jnp.concatenate([cos, cos], axis=-1), jnp.concatenate([-sin, sin], axis=-1)


def kernel(x, norm_gains, hgrn_w_in, hgrn_lb_logits, hgrn_onorm, hgrn_w_out, attn_w_qkv, attn_q_gain,
           attn_k_gain, attn_w_out, ffn_w_gate, ffn_w_up, ffn_w_down, moe_w_router, moe_w_gate, moe_w_up,
           moe_w_down):
    batch, seq, d = x.shape
    t = batch * seq
    heads = d // HEAD_DIM
    hd = HEAD_DIM
    xf = x.reshape(t, d)

    w_in = hgrn_w_in[0]
    w_qvg = jnp.concatenate([w_in[:, :d], w_in[:, 3 * d:5 * d]], axis=1).astype(BF16)
    w_f = w_in[:, d:3 * d].astype(BF16)
    lb = jnp.cumsum(jax.nn.softmax(hgrn_lb_logits.astype(F32), axis=0), axis=0)[0].reshape(1, 2 * d)
    tn = min(1024, d)
    nq = d // tn

    def qvg_epilogue(acc, j, aux, outs, cols):
        is_v = (j >= nq) & (j < 2 * nq)
        outs[0][:, cols] = jnp.where(is_v, acc, _silu(acc)).astype(BF16)

    (qvg,) = norm_matmul(
        "hgrn_qvg_proj", xf, norm_gains[0, 0], w_qvg, qvg_epilogue,
        [jax.ShapeDtypeStruct((t, 3 * d), BF16)], [pl.BlockSpec((1024, tn), lambda i, j: (i, j))], tn=tn)

    def f_epilogue(acc, j, aux, outs, cols):
        lbv = aux[0][:, cols]
        fgate = lbv + (1.0 - lbv) * jax.nn.sigmoid(acc)
        outs[0][:, cols] = jnp.log2(fgate)
        outs[1][:, cols] = (1.0 - fgate).astype(BF16)

    log2f, kk = norm_matmul(
        "hgrn_forget_proj", xf, norm_gains[0, 0], w_f, f_epilogue,
        [jax.ShapeDtypeStruct((t, 2 * d), F32), jax.ShapeDtypeStruct((t, 2 * d), BF16)],
        [pl.BlockSpec((1024, tn), lambda i, j: (i, j)), pl.BlockSpec((1024, tn), lambda i, j: (i, j))],
        aux=(lb,), aux_specs=(pl.BlockSpec((1, tn), lambda i, j: (0, j)),), tn=tn)

    n_exp, _, f_exp = moe_w_gate.shape[1:]
    later_f32 = (moe_w_gate[0].reshape(n_exp * d, f_exp), moe_w_up[0].reshape(n_exp * d, f_exp),
                 moe_w_down[0].reshape(n_exp * f_exp, d), attn_w_qkv[0], attn_w_out[0])
    o_f, o_b, (moe_wg, moe_wu, moe_wd, w_qkv, w_attn_out) = gla_bidirectional(
        qvg, kk, log2f, batch=batch, seq=seq, heads=heads, cast=later_f32)

    tm = 512
    x1 = proj_residual(
        "hgrn_out_proj", (o_f, o_b, qvg, hgrn_onorm[0].reshape(1, d)),
        (pl.BlockSpec((tm, d), lambda i, j: (i, 0)), pl.BlockSpec((tm, d), lambda i, j: (i, 0)),
         pl.BlockSpec((tm, d), lambda i, j: (i, 2)), pl.BlockSpec((1, d), lambda i, j: (0, 0))),
        _hgrn_out_prologue, hgrn_w_out[0].astype(BF16), xf, tm=tm)

    fdim = ffn_w_gate.shape[2]
    fpad = -(-fdim // 512) * 512 - fdim
    wg = jnp.pad(ffn_w_gate[0], ((0, 0), (0, fpad))).astype(BF16)
    wu = jnp.pad(ffn_w_up[0], ((0, 0), (0, fpad))).astype(BF16)
    wd = jnp.pad(ffn_w_down[0], ((0, fpad), (0, 0))).astype(BF16)
    x2 = norm_swiglu_residual(x1, norm_gains[0, 1], wg, wu, wd)

    cos, sin = _rope_tables(seq, hd)
    qg = attn_q_gain[0].reshape(1, hd)
    kg = attn_k_gain[0].reshape(1, hd)
    tm_qkv = 1024
    tn_qkv = 512
    nq_qkv = d // tn_qkv
    pos_blocks = seq // tm_qkv

    def qkv_epilogue(acc, j, aux, outs, cols):
        cos_ref, sin_ref, qg_ref, kg_ref = aux
        o_ref = outs[0]

        def normed_rope(gain, scale):
            for h in range(acc.shape[1] // hd):
                y = _rms_rows(acc[:, h * hd:(h + 1) * hd], gain)
                y = y * cos_ref[...] + pltpu.roll(y, hd // 2, 1) * sin_ref[...]
                o_ref[:, cols.start + h * hd:cols.start + (h + 1) * hd] = y * scale

        @pl.when(j < nq_qkv)
        def _():
            normed_rope(qg_ref[...], hd ** -0.5)

        @pl.when((j >= nq_qkv) & (j < 2 * nq_qkv))
        def _():
            normed_rope(kg_ref[...], 1.0)

        @pl.when(j >= 2 * nq_qkv)
        def _():
            o_ref[:, cols] = acc

    (qkv,) = norm_matmul(
        "attn_qkv_proj", x2, norm_gains[1, 0], w_qkv, qkv_epilogue,
        [jax.ShapeDtypeStruct((t, 3 * d), F32)], [pl.BlockSpec((tm_qkv, tn_qkv), lambda i, j: (i, j))],
        aux=(cos, sin, qg, kg),
        aux_specs=(pl.BlockSpec((tm_qkv, hd), lambda i, j: (i % pos_blocks, 0)),
                   pl.BlockSpec((tm_qkv, hd), lambda i, j: (i % pos_blocks, 0)),
                   pl.BlockSpec((1, hd), lambda i, j: (0, 0)),
                   pl.BlockSpec((1, hd), lambda i, j: (0, 0))),
        tm=tm_qkv, tn=tn_qkv, sub=tn_qkv)

    attn = dilated_attention(qkv, batch=batch, seq=seq, heads=heads)
    x3 = proj_residual("attn_out_proj", (attn,), (pl.BlockSpec((tm, d), lambda i, j: (i, 0)),),
                       _copy_prologue, w_attn_out, x2, tm=tm)

    tb = 512
    w_router = jnp.pad(moe_w_router[0].astype(F32), ((0, 0), (0, LANES - N_EXPERTS)))
    hn3, e_pad, p_pad = moe_router(x3, norm_gains[1, 1], w_router)
    slot_tok, blk_e, n_used, blk_rows, tok_slots = _routing_plan(e_pad[:, :TOP_K], tb)
    y = moe_experts(hn3, slot_tok, blk_e, n_used, blk_rows, moe_wg.reshape(n_exp, d, f_exp),
                    moe_wu.reshape(n_exp, d, f_exp), moe_wd.reshape(n_exp, f_exp, d), tb)
    out = moe_combine(x3, y, tok_slots, p_pad)
    return out.reshape(batch, seq, d)
```

```python
import functools

import jax
import jax.numpy as jnp
from jax import lax
from jax.experimental import pallas as pl
from jax.experimental.pallas import tpu as pltpu

F32 = jnp.float32
BF16 = jnp.bfloat16
EPS = 1e-6
NEG_INF = -1e30
ROPE_THETA = 10000.0

HEAD_DIM = 128
GLA_CHUNK = 64
DIL_BRANCHES = ((128, 1), (512, 4), (2048, 16))
N_EXPERTS = 8
TOP_K = 2

LANES = 128
V7X_VMEM_BYTES = 64 * 1024 * 1024
VMEM_BUDGET = 56 * 1024 * 1024

_NT = (((1,), (1,)), ((), ()))
_TN = (((0,), (0,)), ((), ()))


def _params(semantics, vmem_bytes):
    return pltpu.CompilerParams(dimension_semantics=semantics,
                                vmem_limit_bytes=int(min(vmem_bytes, VMEM_BUDGET)))


def _silu(x):
    return x * jax.nn.sigmoid(x)


def _rms_rows(x, gain):
    ms = jnp.mean(x * x, axis=-1, keepdims=True)
    return x * lax.rsqrt(ms + EPS) * gain


def _norm_matmul_body(x_ref, g_ref, w_ref, *rest, n_aux, epilogue, sub):
    aux, outs, hn_ref = rest[:n_aux], rest[n_aux:-1], rest[-1]
    j = pl.program_id(1)

    @pl.when(j == 0)
    def _():
        hn_ref[...] = _rms_rows(x_ref[...], g_ref[...]).astype(BF16)

    for s in range(w_ref.shape[1] // sub):
        cols = slice(s * sub, (s + 1) * sub)
        acc = jnp.dot(hn_ref[...], w_ref[:, cols], preferred_element_type=F32)
        epilogue(acc, j, aux, outs, cols)


def norm_matmul(name, x, gain, w, epilogue, out_shapes, out_specs, aux=(), aux_specs=(), tm=1024, tn=512,
                sub=256):
    m, d = x.shape
    n = w.shape[1]
    assert m % tm == 0 and n % tn == 0
    out_bytes = sum(2 * tm * tn * jnp.dtype(s.dtype).itemsize for s in out_shapes)
    vmem = 2 * tm * d * 4 + tm * d * 2 + 2 * d * tn * 2 + out_bytes + 4 * tm * tn * 4 + (4 << 20)
    return pl.pallas_call(
        functools.partial(_norm_matmul_body, n_aux=len(aux), epilogue=epilogue, sub=min(sub, tn)),
        grid=(m // tm, n // tn),
        in_specs=[pl.BlockSpec((tm, d), lambda i, j: (i, 0)),
                  pl.BlockSpec((1, d), lambda i, j: (0, 0)),
                  pl.BlockSpec((d, tn), lambda i, j: (0, j)),
                  *aux_specs],
        out_specs=out_specs,
        out_shape=out_shapes,
        scratch_shapes=[pltpu.VMEM((tm, d), BF16)],
        compiler_params=_params(("parallel", "arbitrary"), vmem),
        name=name,
    )(x, gain.reshape(1, d), w, *aux)


def _gla_chunks(chains):
    c = chains[0]["q"].shape[0]
    n_levels = c.bit_length()
    for ch in chains:
        hi = ch["lf2"].astype(BF16)
        lo = (ch["lf2"] - hi.astype(F32)).astype(BF16)
        ch["s"] = jnp.dot(ch["sums"], jnp.concatenate([hi, lo], axis=0), preferred_element_type=F32)
    for ch in chains:
        qd = ch["q"] * jnp.exp2(ch["s"][:c]).astype(BF16)
        ch["o"] = lax.dot_general(qd, ch["st"].astype(BF16), _NT, preferred_element_type=F32)
        ch["a"] = jnp.zeros((c, 2 * c), F32)
    for level in range(0, n_levels, 2):
        for ch in chains:
            qs, ks = [], []
            for l in (level, level + 1):
                if l == 0:
                    qs.append(ch["q"])
                    ks.append(ch["k"])
                elif l < n_levels:
                    e = jnp.exp2(-jnp.abs(ch["s"][l * c:(l + 1) * c])).astype(BF16)
                    qs.append(ch["q"] * e)
                    ks.append(ch["k"] * e)
            if len(ks) == 1:
                ks = ks * 2
            p = lax.dot_general(jnp.concatenate(qs, axis=0), jnp.concatenate(ks, axis=0), _NT,
                                preferred_element_type=F32)
            for i in range(len(qs)):
                ch["a"] = jnp.where(ch["lvl"] == 2 * (level + i) + i, p[i * c:(i + 1) * c, :], ch["a"])
    outs = []
    for ch in chains:
        g = ch["s"][:c]
        g_tot = g[c - 1:c, :] if ch["fwd"] else g[0:1, :]
        vv = jnp.concatenate([ch["v"], ch["v"]], axis=0)
        o = ch["o"] + jnp.dot(ch["a"].astype(BF16), vv, preferred_element_type=F32)
        kd = ch["k"] * jnp.exp2(g_tot - g).astype(BF16)
        st_new = (ch["st"] * jnp.exp2(g_tot)
                  + lax.dot_general(ch["v"], kd, _TN, preferred_element_type=F32))
        outs.append((o, st_new))
    return outs


def _gla_tables(c, fwd):
    ti = lax.broadcasted_iota(jnp.int32, (c, c), 0)
    ui = lax.broadcasted_iota(jnp.int32, (c, c), 1)

    def cum(row):
        return ((ui <= row) if fwd else (ui >= row)).astype(jnp.int32)

    blocks = [cum(ti)]
    half = 1
    while half < c:
        boundary = (ti & -(2 * half)) + (half - 1 if fwd else half)
        blocks.append(cum(ti) - cum(boundary))
        half *= 2
    x = ti ^ ui
    top_bit = sum((x >= (1 << b)).astype(jnp.int32) for b in range(1, c.bit_length() - 1))
    lvl = jnp.where(ti == ui, 0, jnp.where((ui < ti) if fwd else (ui > ti), 1 + top_bit, -1))
    sums = jnp.concatenate(blocks, axis=0).astype(F32).astype(BF16)
    codes = jnp.concatenate([2 * lvl, 2 * lvl + 1], axis=1)
    return jnp.concatenate([sums, sums], axis=1), codes


def _gla_body(qf_ref, kf_ref, vf_ref, lf_ref, qb_ref, kb_ref, vb_ref, lb_ref, *rest, chunk, n_cast):
    cast_in, (of_ref, ob_ref) = rest[:n_cast], rest[n_cast:n_cast + 2]
    cast_out, (st_ref, sums_ref, lvl_ref) = rest[n_cast + 2:2 * n_cast + 2], rest[2 * n_cast + 2:]
    rows, width = qf_ref.shape
    nch = rows // chunk
    c = chunk

    for src, dst in zip(cast_in, cast_out):
        dst[...] = src[...].astype(dst.dtype)

    @pl.when(pl.program_id(2) == 0)
    def _():
        st_ref[...] = jnp.zeros_like(st_ref)

    dirs = []
    for d, (fwd, refs) in enumerate(((True, (qf_ref, kf_ref, vf_ref, lf_ref, of_ref)),
                                     (False, (qb_ref, kb_ref, vb_ref, lb_ref, ob_ref)))):
        sums, lvl = _gla_tables(c, fwd)
        sums_ref[d] = sums
        lvl_ref[d] = lvl
        dirs.append((fwd, refs))

    def one_chunk(ci, carry):
        chains, dests = [], []
        for d, (fwd, (q_ref, k_ref, v_ref, l_ref, o_ref)) in enumerate(dirs):
            r0 = pl.multiple_of((ci if fwd else nch - 1 - ci) * c, c)
            for h in range(width // HEAD_DIM):
                hs = slice(h * HEAD_DIM, (h + 1) * HEAD_DIM)
                chains.append(dict(q=q_ref[pl.ds(r0, c), hs], k=k_ref[pl.ds(r0, c), hs],
                                   v=v_ref[pl.ds(r0, c), hs], lf2=l_ref[pl.ds(r0, c), hs],
                                   st=st_ref[d, h], lvl=lvl_ref[d], fwd=fwd, sums=sums_ref[d]))
                dests.append((o_ref, r0, hs, d, h))
        for (o, st_new), (o_ref, r0, hs, d, h) in zip(_gla_chunks(chains), dests):
            o_ref[pl.ds(r0, c), hs] = o.astype(o_ref.dtype)
            st_ref[d, h] = st_new
        return carry

    lax.fori_loop(0, nch, one_chunk, 0)


def _cast_block_rows(rows, steps):
    br = 16
    while rows % br or rows // br > steps:
        br += 16
    return br


def gla_bidirectional(qvg, kk, log2f, *, batch, seq, heads, cast=(), heads_per_step=4, rows_per_step=512):
    t = batch * seq
    hg = heads_per_step
    width = hg * HEAD_DIM
    rb = rows_per_step
    ns = seq // rb
    ng = heads // hg

    def spec(fwd, seg):
        return pl.BlockSpec((rb, width),
                            lambda b, g, s: (b * ns + (s if fwd else ns - 1 - s), seg * ng + g))

    cast_specs = []
    for arr in cast:
        br = _cast_block_rows(arr.shape[0], batch * ng * ns)
        last = arr.shape[0] // br - 1
        cast_specs.append(pl.BlockSpec(
            (br, arr.shape[1]), lambda b, g, s, last=last: (jnp.minimum((b * ng + g) * ns + s, last), 0)))
    cast_bytes = sum(2 * s.block_shape[0] * s.block_shape[1] * (4 + 2) for s in cast_specs)
    outs = pl.pallas_call(
        functools.partial(_gla_body, chunk=GLA_CHUNK, n_cast=len(cast)),
        grid=(batch, ng, ns),
        in_specs=[spec(True, 0), spec(True, 0), spec(True, 1), spec(True, 0),
                  spec(False, 0), spec(False, 1), spec(False, 1), spec(False, 1), *cast_specs],
        out_specs=[spec(True, 0), spec(False, 0), *cast_specs],
        out_shape=[jax.ShapeDtypeStruct((t, heads * HEAD_DIM), BF16)] * 2
                  + [jax.ShapeDtypeStruct(arr.shape, BF16) for arr in cast],
        scratch_shapes=[pltpu.VMEM((2, hg, HEAD_DIM, HEAD_DIM), F32),
                        pltpu.VMEM((2, GLA_CHUNK * GLA_CHUNK.bit_length(), 2 * GLA_CHUNK), BF16),
                        pltpu.VMEM((2, GLA_CHUNK, 2 * GLA_CHUNK), jnp.int32)],
        compiler_params=_params(("arbitrary", "arbitrary", "arbitrary"), cast_bytes + (24 << 20)),
        name="gla_bidir",
    )(qvg, kk, qvg, log2f, qvg, kk, qvg, log2f, *cast)
    return outs[0], outs[1], outs[2:]


def _proj_residual_body(*refs, n_pro, prologue):
    pro, (w_ref, x_ref, o_ref, y_ref) = refs[:n_pro], refs[n_pro:]

    @pl.when(pl.program_id(1) == 0)
    def _():
        prologue(pro, y_ref)

    o_ref[...] = x_ref[...] + jnp.dot(y_ref[...], w_ref[...], preferred_element_type=F32)


def proj_residual(name, pro_inputs, pro_specs, prologue, w, xres, tm=512, tn=1024):
    m, n = xres.shape
    tn = min(tn, n)
    kdim = w.shape[0]
    pro_bytes = sum(2 * tm * kdim * jnp.dtype(a.dtype).itemsize for a in pro_inputs)
    vmem = pro_bytes + tm * kdim * 2 + 2 * kdim * tn * 2 + 4 * tm * tn * 4 + 4 * tm * kdim * 4 + (4 << 20)
    return pl.pallas_call(
        functools.partial(_proj_residual_body, n_pro=len(pro_inputs), prologue=prologue),
        grid=(m // tm, n // tn),
        in_specs=[*pro_specs,
                  pl.BlockSpec((kdim, tn), lambda i, j: (0, j)),
                  pl.BlockSpec((tm, tn), lambda i, j: (i, j))],
        out_specs=pl.BlockSpec((tm, tn), lambda i, j: (i, j)),
        out_shape=jax.ShapeDtypeStruct((m, n), F32),
        scratch_shapes=[pltpu.VMEM((tm, kdim), BF16)],
        compiler_params=_params(("parallel", "arbitrary"), vmem),
        name=name,
    )(*pro_inputs, w, xres)


def _hgrn_out_prologue(pro, y_ref):
    of_ref, ob_ref, gate_ref, gain_ref = pro
    for h in range(of_ref.shape[1] // HEAD_DIM):
        hs = slice(h * HEAD_DIM, (h + 1) * HEAD_DIM)
        o = of_ref[:, hs].astype(F32) + ob_ref[:, hs].astype(F32)
        y = _rms_rows(o, gain_ref[:, hs]) * gate_ref[:, hs].astype(F32)
        y_ref[:, hs] = y.astype(BF16)


def _copy_prologue(pro, y_ref):
    y_ref[...] = pro[0][...]


def _swiglu_body(x_ref, g_ref, wg_ref, wu_ref, wd_ref, o_ref, hn_ref):
    @pl.when(pl.program_id(1) == 0)
    def _():
        x = x_ref[...]
        hn_ref[...] = _rms_rows(x, g_ref[...]).astype(BF16)
        o_ref[...] = x

    hn = hn_ref[...]
    a = jnp.dot(hn, wg_ref[...], preferred_element_type=F32)
    u = jnp.dot(hn, wu_ref[...], preferred_element_type=F32)
    h = (_silu(a) * u).astype(BF16)
    o_ref[...] += jnp.dot(h, wd_ref[...], preferred_element_type=F32)


def norm_swiglu_residual(x, gain, wg, wu, wd, tm=512, tf=512):
    m, d = x.shape
    f = wg.shape[1]
    assert m % tm == 0 and f % tf == 0
    vmem = 4 * tm * d * 4 + tm * d * 2 + 2 * (2 * d * tf + tf * d) * 2 + 3 * tm * tf * 4 + 2 * tm * d * 4 + (4 << 20)
    return pl.pallas_call(
        _swiglu_body,
        grid=(m // tm, f // tf),
        in_specs=[pl.BlockSpec((tm, d), lambda i, j: (i, 0)),
                  pl.BlockSpec((1, d), lambda i, j: (0, 0)),
                  pl.BlockSpec((d, tf), lambda i, j: (0, j)),
                  pl.BlockSpec((d, tf), lambda i, j: (0, j)),
                  pl.BlockSpec((tf, d), lambda i, j: (j, 0))],
        out_specs=pl.BlockSpec((tm, d), lambda i, j: (i, 0)),
        out_shape=jax.ShapeDtypeStruct((m, d), F32),
        scratch_shapes=[pltpu.VMEM((tm, d), BF16)],
        compiler_params=_params(("parallel", "arbitrary"), vmem),
        name="dense_swiglu",
    )(x, gain.reshape(1, d), wg, wu, wd)


def _dilated_attn_body(*refs, hg, branches, tq):
    q_refs, k_refs, v_refs = (refs[i * hg:(i + 1) * hg] for i in range(3))
    o_ref = refs[3 * hg]
    acc_refs, m_refs, l_refs = (refs[3 * hg + 1 + i * hg:3 * hg + 1 + (i + 1) * hg] for i in range(3))
    bias_ref = refs[6 * hg + 1]
    seq = q_refs[0].shape[0]
    steps = branches[0][1]
    tq_max = min(tq, seq)
    wk_max = min(seq, tq_max + 2 * steps)

    d = (lax.broadcasted_iota(jnp.int32, (tq_max, wk_max), 1)
         - lax.broadcasted_iota(jnp.int32, (tq_max, wk_max), 0))
    for shift in range(3):
        bias_ref[shift] = jnp.where(jnp.abs(d - shift * steps) <= steps, 0.0, NEG_INF)

    for bi, (r, br_steps) in enumerate(branches):
        assert br_steps == steps
        n = seq // r
        tqb = min(tq, n)
        wk = min(n, tqb + 2 * steps)
        nqb = n // tqb

        def block(idx, carry, bi=bi, r=r, n=n, tqb=tqb, wk=wk, nqb=nqb):
            rho = idx // nqb
            q0 = (idx - rho * nqb) * tqb
            k0 = jnp.clip(q0 - steps, 0, n - wk)
            bias = bias_ref[(q0 - k0) // steps, :tqb, :wk]

            def rows(c0, cnt):
                if r == 1:
                    return pl.ds(pl.multiple_of(c0, steps), cnt)
                return pl.ds(rho + r * c0, cnt, stride=r)

            s = [lax.dot_general(q_refs[h][rows(q0, tqb), :].astype(BF16),
                                 k_refs[h][rows(k0, wk), :].astype(BF16), _NT,
                                 preferred_element_type=F32) + bias for h in range(hg)]
            m_new = [jnp.broadcast_to(jnp.max(s[h], axis=-1, keepdims=True), (tqb, HEAD_DIM))
                     for h in range(hg)]
            p = [jnp.exp(s[h] - m_new[h][:, :1]) for h in range(hg)]
            l_new = [jnp.broadcast_to(jnp.sum(p[h], axis=-1, keepdims=True), (tqb, HEAD_DIM))
                     for h in range(hg)]
            acc_new = [jnp.dot(p[h].astype(BF16), v_refs[h][rows(k0, wk), :].astype(BF16),
                               preferred_element_type=F32) for h in range(hg)]
            for h in range(hg):
                if bi > 0:
                    m_old = m_refs[h][rows(q0, tqb), :]
                    m_all = jnp.maximum(m_old, m_new[h])
                    w_old = jnp.exp(m_old - m_all)
                    w_new = jnp.exp(m_new[h] - m_all)
                    acc_new[h] = acc_refs[h][rows(q0, tqb), :] * w_old + acc_new[h] * w_new
                    l_new[h] = l_refs[h][rows(q0, tqb), :] * w_old + l_new[h] * w_new
                    m_new[h] = m_all
                acc_refs[h][rows(q0, tqb), :] = acc_new[h]
                l_refs[h][rows(q0, tqb), :] = l_new[h]
                m_refs[h][rows(q0, tqb), :] = m_new[h]
            return carry

        lax.fori_loop(0, r * nqb, block, 0)

    for h in range(hg):
        o_ref[:, h * HEAD_DIM:(h + 1) * HEAD_DIM] = (acc_refs[h][...] / l_refs[h][...]).astype(o_ref.dtype)


def dilated_attention(qkv, *, batch, seq, heads, heads_per_step=2, tq=256):
    t = batch * seq
    hd = HEAD_DIM
    hg = heads_per_step
    branches = tuple((dil, window // (2 * dil)) for window, dil in DIL_BRANCHES)
    steps = branches[0][1]
    tq_max = min(tq, seq)
    specs = lambda part: [pl.BlockSpec((seq, hd), lambda b, g, h=h: (b, part * heads + g * hg + h))
                          for h in range(hg)]
    return pl.pallas_call(
        functools.partial(_dilated_attn_body, hg=hg, branches=branches, tq=tq),
        grid=(batch, heads // hg),
        in_specs=specs(0) + specs(1) + specs(2),
        out_specs=pl.BlockSpec((seq, hg * hd), lambda b, g: (b, g)),
        out_shape=jax.ShapeDtypeStruct((t, heads * hd), BF16),
        scratch_shapes=([pltpu.VMEM((seq, hd), F32)] * (3 * hg)
                        + [pltpu.VMEM((3, tq_max, min(seq, tq_max + 2 * steps)), F32)]),
        compiler_params=_params(("parallel", "parallel"),
                                hg * seq * hd * (2 * 3 * 4 + 2 * 2 + 3 * 4) + (16 << 20)),
        name="dilated_attn",
    )(*([qkv] * (3 * hg)))


def _router_body(x_ref, g_ref, wr_ref, hn_ref, e_ref, p_ref):
    hn = _rms_rows(x_ref[...], g_ref[...])
    hn_ref[...] = hn
    logits = jnp.dot(hn, wr_ref[...], precision=lax.Precision.HIGHEST, preferred_element_type=F32)
    lane = lax.broadcasted_iota(jnp.int32, logits.shape, 1)
    logits = jnp.where(lane < N_EXPERTS, logits, -jnp.inf)
    m1 = jnp.max(logits, axis=-1, keepdims=True)
    i1 = jnp.min(jnp.where(logits == m1, lane, LANES), axis=-1, keepdims=True)
    rest = jnp.where(lane == i1, -jnp.inf, logits)
    m2 = jnp.max(rest, axis=-1, keepdims=True)
    i2 = jnp.min(jnp.where(rest == m2, lane, LANES), axis=-1, keepdims=True)
    e2 = jnp.exp(m2 - m1)
    den = 1.0 + e2
    e_ref[...] = jnp.where(lane == 0, i1, jnp.where(lane == 1, i2, 0))
    p_ref[...] = jnp.where(lane == 0, 1.0 / den, jnp.where(lane == 1, e2 / den, 0.0))


def moe_router(x, gain, w_router_padded, tm=512):
    m, d = x.shape
    return pl.pallas_call(
        _router_body,
        grid=(m // tm,),
        in_specs=[pl.BlockSpec((tm, d), lambda i: (i, 0)),
                  pl.BlockSpec((1, d), lambda i: (0, 0)),
                  pl.BlockSpec((d, LANES), lambda i: (0, 0))],
        out_specs=[pl.BlockSpec((tm, d), lambda i: (i, 0)),
                   pl.BlockSpec((tm, LANES), lambda i: (i, 0)),
                   pl.BlockSpec((tm, LANES), lambda i: (i, 0))],
        out_shape=[jax.ShapeDtypeStruct((m, d), F32),
                   jax.ShapeDtypeStruct((m, LANES), jnp.int32),
                   jax.ShapeDtypeStruct((m, LANES), F32)],
        compiler_params=_params(("parallel",), 6 * tm * d * 4 + (8 << 20)),
        name="moe_router",
    )(x, gain.reshape(1, d), w_router_padded)


def _gather_rows_per_step(tb, nf):
    per_step = -(-tb // nf)
    while (per_step * nf) % 8:
        per_step += 1
    return per_step


def _expert_body(blk_e_ref, n_used_ref, blk_rows_ref, tok_ref, tok_next_ref, hn_hbm, wg_ref, wu_ref, wd_ref,
                 y_ref, xf_ref, xb_ref, sem, *, nf, nblk):
    b = pl.program_id(0)
    f = pl.program_id(1)
    tb = y_ref.shape[0]
    n_used = n_used_ref[0]
    used = b < n_used
    slot = lax.rem(b, 2)
    per_step = _gather_rows_per_step(tb, nf)

    def row_copy(toks, j, dst_slot):
        return pltpu.make_async_copy(hn_hbm.at[pl.ds(toks[0, jnp.minimum(j, tb - 1)], 1), :],
                                     xf_ref.at[dst_slot, pl.ds(j, 1), :], sem.at[dst_slot])

    def wait_gather(dst_slot):
        rows = per_step * nf
        pltpu.make_async_copy(hn_hbm.at[pl.ds(0, rows), :], xf_ref.at[dst_slot], sem.at[dst_slot]).wait()

    @pl.when((b == 0) & (f == 0))
    def _():
        def start(j, c):
            row_copy(tok_ref, j, 0).start()
            return c
        lax.fori_loop(0, per_step * nf, start, 0)

    @pl.when(used & (f == 0))
    def _():
        wait_gather(slot)
        xb_ref[...] = xf_ref[slot, :tb, :].astype(BF16)
        y_ref[...] = jnp.zeros_like(y_ref)

    @pl.when(~used & (f == 0))
    def _():
        y_ref[...] = jnp.zeros_like(y_ref)

    def compute(rows):
        for j in range(per_step):
            row_copy(tok_next_ref, f * per_step + j, 1 - slot).start(priority=1)
        xb = xb_ref[:rows, :]
        a = jnp.dot(xb, wg_ref[...], preferred_element_type=F32)
        u = jnp.dot(xb, wu_ref[...], preferred_element_type=F32)
        h = (_silu(a) * u).astype(BF16)
        y_ref[:rows, :] += jnp.dot(h, wd_ref[...], preferred_element_type=F32)

    live = blk_rows_ref[b]

    @pl.when(used & (live > tb // 2))
    def _():
        compute(tb)

    @pl.when(used & (live <= tb // 2))
    def _():
        compute(tb // 2)

    @pl.when(used & (f == nf - 1) & (b + 1 >= n_used))
    def _():
        wait_gather(1 - slot)


def moe_experts(hn, slot_tok, blk_e, n_used, blk_rows, wg, wu, wd, tb, tf=512):
    t, d = hn.shape
    p = slot_tok.shape[0]
    nblk = p // tb
    fdim = wg.shape[2]
    nf = fdim // tf

    def live(b, n_used_ref):
        return jnp.minimum(b, n_used_ref[0] - 1)

    def wmap_cols(b, f, blk_e_ref, n_used_ref, blk_rows_ref):
        return (blk_e_ref[live(b, n_used_ref)], 0, jnp.where(b < n_used_ref[0], f, nf - 1))

    def wmap_rows(b, f, blk_e_ref, n_used_ref, blk_rows_ref):
        return (blk_e_ref[live(b, n_used_ref)], jnp.where(b < n_used_ref[0], f, nf - 1), 0)

    vmem = (tb * d * (2 * 4 + 2) + 2 * tb * d * 4 + 2 * (2 * d * tf + tf * d) * 2 + 3 * tb * tf * 4
            + tb * d * 4 + (4 << 20))
    toks = slot_tok.reshape(nblk, 1, tb)
    grid_spec = pltpu.PrefetchScalarGridSpec(
        num_scalar_prefetch=3,
        grid=(nblk, nf),
        in_specs=[pl.BlockSpec((None, 1, tb), lambda b, f, *_: (b, 0, 0), memory_space=pltpu.SMEM),
                  pl.BlockSpec((None, 1, tb), lambda b, f, *_: (jnp.minimum(b + 1, nblk - 1), 0, 0),
                               memory_space=pltpu.SMEM),
                  pl.BlockSpec(memory_space=pl.ANY),
                  pl.BlockSpec((None, d, tf), wmap_cols),
                  pl.BlockSpec((None, d, tf), wmap_cols),
                  pl.BlockSpec((None, tf, d), wmap_rows)],
        out_specs=pl.BlockSpec((tb, d), lambda b, f, *_: (b, 0)),
        scratch_shapes=[pltpu.VMEM((2, _gather_rows_per_step(tb, nf) * nf, d), F32), pltpu.VMEM((tb, d), BF16),
                        pltpu.SemaphoreType.DMA((2,))],
    )
    return pl.pallas_call(
        functools.partial(_expert_body, nf=nf, nblk=nblk),
        grid_spec=grid_spec,
        out_shape=jax.ShapeDtypeStruct((p, d), F32),
        compiler_params=_params(("arbitrary", "arbitrary"), vmem),
        name="moe_experts",
    )(blk_e, n_used, blk_rows, toks, toks, hn, wg, wu, wd)


def _combine_body(slot_ref, slot_next_ref, x_ref, p_ref, y_hbm, o_ref, buf_ref, sem):
    i = pl.program_id(0)
    nt = pl.num_programs(0)
    tm = x_ref.shape[0]
    cur = lax.rem(i, 2)

    def row_copy(slots, j, half):
        k, r = divmod(j, tm)
        return pltpu.make_async_copy(y_hbm.at[pl.ds(slots[0, j], 1), :],
                                     buf_ref.at[half, k, pl.ds(r, 1), :], sem.at[half])

    def wait_tile(half):
        for k in range(TOP_K):
            pltpu.make_async_copy(y_hbm.at[pl.ds(0, tm), :], buf_ref.at[half, k], sem.at[half]).wait()

    @pl.when(i == 0)
    def _():
        for j in range(TOP_K * tm):
            row_copy(slot_ref, j, 0).start(priority=j % 2)

    for j in range(TOP_K * tm):
        row_copy(slot_next_ref, j, 1 - cur).start(priority=j % 2)
    wait_tile(cur)
    gates = p_ref[...]
    o_ref[...] = x_ref[...] + (gates[:, 0:1] * buf_ref[cur, 0] + gates[:, 1:2] * buf_ref[cur, 1])

    @pl.when(i == nt - 1)
    def _():
        wait_tile(1 - cur)


def moe_combine(x, y, tok_slots, gates, tm=256):
    t, d = x.shape
    nt = t // tm
    slots = tok_slots.reshape(nt, tm, TOP_K).transpose(0, 2, 1).reshape(nt, 1, TOP_K * tm)
    slot_spec = lambda index_map: pl.BlockSpec((None, 1, TOP_K * tm), index_map, memory_space=pltpu.SMEM)
    return pl.pallas_call(
        _combine_body,
        grid=(nt,),
        in_specs=[slot_spec(lambda i: (i, 0, 0)),
                  slot_spec(lambda i: (jnp.minimum(i + 1, nt - 1), 0, 0)),
                  pl.BlockSpec((tm, d), lambda i: (i, 0)),
                  pl.BlockSpec((tm, LANES), lambda i: (i, 0)),
                  pl.BlockSpec(memory_space=pl.ANY)],
        out_specs=pl.BlockSpec((tm, d), lambda i: (i, 0)),
        out_shape=jax.ShapeDtypeStruct((t, d), F32),
        scratch_shapes=[pltpu.VMEM((2, TOP_K, tm, d), F32), pltpu.SemaphoreType.DMA((2,))],
        compiler_params=_params(("arbitrary",), (4 + 2 * TOP_K) * tm * d * 4 + (4 << 20)),
        name="moe_combine",
    )(slots, slots, x, gates, y)


def _routing_plan(top_e, tb):
    t = top_e.shape[0]
    n = t * TOP_K
    e_flat = top_e.reshape(n)
    onehot = (e_flat[:, None] == jnp.arange(N_EXPERTS, dtype=jnp.int32)[None, :]).astype(jnp.int32)
    rank = jnp.take_along_axis(jnp.cumsum(onehot, axis=0) - onehot, e_flat[:, None], axis=1)[:, 0]
    counts = jnp.sum(onehot, axis=0)
    padded = (counts + tb - 1) // tb * tb
    pend = jnp.cumsum(padded)
    dest = (pend - padded)[e_flat] + rank
    nblk = -(-n // tb) + N_EXPERTS
    p = nblk * tb
    tok_flat = jnp.arange(n, dtype=jnp.int32) // TOP_K
    slot_tok = jnp.zeros((p,), jnp.int32).at[dest].set(tok_flat)
    blk_start = jnp.arange(nblk, dtype=pend.dtype) * tb
    blk_e = jnp.minimum(jnp.searchsorted(pend, blk_start, side='right'), N_EXPERTS - 1).astype(jnp.int32)
    n_used = (pend[-1] // tb).astype(jnp.int32).reshape(1)
    blk_rows = jnp.clip(counts[blk_e] - (blk_start - (pend - padded)[blk_e]), 0, tb).astype(jnp.int32)
    return slot_tok, blk_e, n_used, blk_rows, dest.reshape(t, TOP_K).astype(jnp.int32)


def _rope_tables(seq, hd):
    half = hd // 2
    inv_freq = ROPE_THETA ** (-jnp.arange(half, dtype=F32) * 2.0 / hd)
    ang = jnp.arange(seq, dtype=F32)[:, None] * inv_freq[None, :]
    cos, sin = jnp.cos(ang), jnp.sin(ang)
    return jnp.concatenate([cos, cos], axis=-1), jnp.concatenate([-sin, sin], axis=-1)


def kernel(x, norm_gains, hgrn_w_in, hgrn_lb_logits, hgrn_onorm, hgrn_w_out, attn_w_qkv, attn_q_gain,
           attn_k_gain, attn_w_out, ffn_w_gate, ffn_w_up, ffn_w_down, moe_w_router, moe_w_gate, moe_w_up,
           moe_w_down):
    batch, seq, d = x.shape
    t = batch * seq
    heads = d // HEAD_DIM
    hd = HEAD_DIM
    xf = x.reshape(t, d)

    w_in = hgrn_w_in[0]
    w_qvg = jnp.concatenate([w_in[:, :d], w_in[:, 3 * d:5 * d]], axis=1).astype(BF16)
    w_f = w_in[:, d:3 * d].astype(BF16)
    lb = jnp.cumsum(jax.nn.softmax(hgrn_lb_logits.astype(F32), axis=0), axis=0)[0].reshape(1, 2 * d)
    tn = min(1024, d)
    nq = d // tn

    def qvg_epilogue(acc, j, aux, outs, cols):
        is_v = (j >= nq) & (j < 2 * nq)
        outs[0][:, cols] = jnp.where(is_v, acc, _silu(acc)).astype(BF16)

    (qvg,) = norm_matmul(
        "hgrn_qvg_proj", xf, norm_gains[0, 0], w_qvg, qvg_epilogue,
        [jax.ShapeDtypeStruct((t, 3 * d), BF16)], [pl.BlockSpec((1024, tn), lambda i, j: (i, j))], tn=tn)

    def f_epilogue(acc, j, aux, outs, cols):
        lbv = aux[0][:, cols]
        fgate = lbv + (1.0 - lbv) * jax.nn.sigmoid(acc)
        outs[0][:, cols] = jnp.log2(fgate)
        outs[1][:, cols] = (1.0 - fgate).astype(BF16)

    log2f, kk = norm_matmul(
        "hgrn_forget_proj", xf, norm_gains[0, 0], w_f, f_epilogue,
        [jax.ShapeDtypeStruct((t, 2 * d), F32), jax.ShapeDtypeStruct((t, 2 * d), BF16)],
        [pl.BlockSpec((1024, tn), lambda i, j: (i, j)), pl.BlockSpec((1024, tn), lambda i, j: (i, j))],
        aux=(lb,), aux_specs=(pl.BlockSpec((1, tn), lambda i, j: (0, j)),), tn=tn)

    n_exp, _, f_exp = moe_w_gate.shape[1:]
    later_f32 = (moe_w_gate[0].reshape(n_exp * d, f_exp), moe_w_up[0].reshape(n_exp * d, f_exp),
                 moe_w_down[0].reshape(n_exp * f_exp, d), attn_w_qkv[0], attn_w_out[0])
    o_f, o_b, (moe_wg, moe_wu, moe_wd, w_qkv, w_attn_out) = gla_bidirectional(
        qvg, kk, log2f, batch=batch, seq=seq, heads=heads, cast=later_f32)

    tm = 512
    x1 = proj_residual(
        "hgrn_out_proj", (o_f, o_b, qvg, hgrn_onorm[0].reshape(1, d)),
        (pl.BlockSpec((tm, d), lambda i, j: (i, 0)), pl.BlockSpec((tm, d), lambda i, j: (i, 0)),
         pl.BlockSpec((tm, d), lambda i, j: (i, 2)), pl.BlockSpec((1, d), lambda i, j: (0, 0))),
        _hgrn_out_prologue, hgrn_w_out[0].astype(BF16), xf, tm=tm)

    fdim = ffn_w_gate.shape[2]
    fpad = -(-fdim // 512) * 512 - fdim
    wg = jnp.pad(ffn_w_gate[0], ((0, 0), (0, fpad))).astype(BF16)
    wu = jnp.pad(ffn_w_up[0], ((0, 0), (0, fpad))).astype(BF16)
    wd = jnp.pad(ffn_w_down[0], ((0, fpad), (0, 0))).astype(BF16)
    x2 = norm_swiglu_residual(x1, norm_gains[0, 1], wg, wu, wd)

    cos, sin = _rope_tables(seq, hd)
    qg = attn_q_gain[0].reshape(1, hd)
    kg = attn_k_gain[0].reshape(1, hd)
    tm_qkv = 1024
    tn_qkv = 512
    nq_qkv = d // tn_qkv
    pos_blocks = seq // tm_qkv

    def qkv_epilogue(acc, j, aux, outs, cols):
        cos_ref, sin_ref, qg_ref, kg_ref = aux
        o_ref = outs[0]

        def normed_rope(gain, scale):
            for h in range(acc.shape[1] // hd):
                y = _rms_rows(acc[:, h * hd:(h + 1) * hd], gain)
                y = y * cos_ref[...] + pltpu.roll(y, hd // 2, 1) * sin_ref[...]
                o_ref[:, cols.start + h * hd:cols.start + (h + 1) * hd] = y * scale

        @pl.when(j < nq_qkv)
        def _():
            normed_rope(qg_ref[...], hd ** -0.5)

        @pl.when((j >= nq_qkv) & (j < 2 * nq_qkv))
        def _():
            normed_rope(kg_ref[...], 1.0)

        @pl.when(j >= 2 * nq_qkv)
        def _():
            o_ref[:, cols] = acc

    (qkv,) = norm_matmul(
        "attn_qkv_proj", x2, norm_gains[1, 0], w_qkv, qkv_epilogue,
        [jax.ShapeDtypeStruct((t, 3 * d), F32)], [pl.BlockSpec((tm_qkv, tn_qkv), lambda i, j: (i, j))],
        aux=(cos, sin, qg, kg),
        aux_specs=(pl.BlockSpec((tm_qkv, hd), lambda i, j: (i % pos_blocks, 0)),
                   pl.BlockSpec((tm_qkv, hd), lambda i, j: (i % pos_blocks, 0)),
                   pl.BlockSpec((1, hd), lambda i, j: (0, 0)),
                   pl.BlockSpec((1, hd), lambda i, j: (0, 0))),
        tm=tm_qkv, tn=tn_qkv, sub=tn_qkv)

    attn = dilated_attention(qkv, batch=batch, seq=seq, heads=heads)
    x3 = proj_residual("attn_out_proj", (attn,), (pl.BlockSpec((tm, d), lambda i, j: (i, 0)),),
                       _copy_prologue, w_attn_out, x2, tm=tm)

    tb = 512
    w_router = jnp.pad(moe_w_router[0].astype(F32), ((0, 0), (0, LANES - N_EXPERTS)))
    hn3, e_pad, p_pad = moe_router(x3, norm_gains[1, 1], w_router)
    slot_tok, blk_e, n_used, blk_rows, tok_slots = _routing_plan(e_pad[:, :TOP_K], tb)
    y = moe_experts(hn3, slot_tok, blk_e, n_used, blk_rows, moe_wg.reshape(n_exp, d, f_exp),
                    moe_wu.reshape(n_exp, d, f_exp), moe_wd.reshape(n_exp, f_exp, d), tb)
    out = moe_combine(x3, y, tok_slots, p_pad)
    return out.reshape(batch, seq, d)
```
